```python
import jax
import jax.numpy as jnp
from jax import lax
import numpy as np

D_MODEL = 1024
BATCH = 2
SEQ = 16384
DEPTH = 4

GRID_W = 64
CTX_LEN = 256

N_MIXERS = 3
N_ATTN_LAYERS = len(range(0, DEPTH, N_MIXERS))
N_LRU_LAYERS = len(range(1, DEPTH, N_MIXERS))
N_CONV_LAYERS = len(range(2, DEPTH, N_MIXERS))

HEAD_DIM = 64
N_HEADS = D_MODEL // HEAD_DIM
N_KV_HEADS = 4
GROUP = N_HEADS // N_KV_HEADS
WINDOW = 128
BLOCK = 128
ROPE_BASE = 10000.0

D_RNN = D_MODEL
LRU_BLOCK_W = 256
N_LRU_BLOCKS = D_RNN // LRU_BLOCK_W
LRU_CONV_W = 4
LRU_CONV_PAD = (LRU_CONV_W // 2, LRU_CONV_W - 1 - LRU_CONV_W // 2)
LRU_C = 8.0

CONV_K = 31
CONV_PAD = (CONV_K // 2, CONV_K // 2)

N_EXPERTS = 16
D_FF_EXPERT = 2 * D_MODEL
EC_CAPACITY_FACTOR = 2

DEEPNORM_ALPHA = (2 * DEPTH) ** 0.25
DEEPNORM_BETA = (8 * DEPTH) ** -0.25
LN_EPS = 1e-5
NEG_INF = -1e30

kernel_name = "hybrid_interleaved_diffusion_trunk"


def layer_norm(x, g, b):
    xf = x.astype(jnp.float32)
    mu = jnp.mean(xf, axis=-1, keepdims=True)
    var = jnp.mean(jnp.square(xf - mu), axis=-1, keepdims=True)
    y = (xf - mu) * lax.rsqrt(var + LN_EPS) * g.astype(jnp.float32) + b.astype(jnp.float32)
    return y.astype(x.dtype)


def depthwise_conv(x, w, b, pad):
    y = lax.conv_general_dilated(
        x, w[:, None, :], window_strides=(1,), padding=[pad],
        dimension_numbers=("NWC", "WIO", "NWC"), feature_group_count=x.shape[-1])
    return y + b


def axial_rope_tables(n):
    rows = n // GRID_W
    row = jnp.broadcast_to(jnp.arange(rows, dtype=jnp.float32)[:, None], (rows, GRID_W)).reshape(-1)
    col = jnp.broadcast_to(jnp.arange(GRID_W, dtype=jnp.float32)[None, :], (rows, GRID_W)).reshape(-1)
    n_freq = HEAD_DIM // 4
    inv = ROPE_BASE ** (-jnp.arange(n_freq, dtype=jnp.float32) / n_freq)
    ang = jnp.concatenate([row[:, None] * inv, col[:, None] * inv], axis=-1)
    return jnp.cos(ang), jnp.sin(ang)


def apply_rope(t, cos, sin):
    tf = t.astype(jnp.float32)
    t1, t2 = tf[..., :HEAD_DIM // 2], tf[..., HEAD_DIM // 2:]
    cs, sn = cos[None, :, None, :], sin[None, :, None, :]
    return jnp.concatenate([t1 * cs - t2 * sn, t2 * cs + t1 * sn], axis=-1).astype(t.dtype)


def split_qkv(t, w_qkv):
    bsz, n, _ = t.shape
    q_w, kv_w = N_HEADS * HEAD_DIM, N_KV_HEADS * HEAD_DIM
    z = t @ w_qkv
    q = z[..., :q_w].reshape(bsz, n, N_HEADS, HEAD_DIM)
    k = z[..., q_w:q_w + kv_w].reshape(bsz, n, N_KV_HEADS, HEAD_DIM)
    v = z[..., q_w + kv_w:].reshape(bsz, n, N_KV_HEADS, HEAD_DIM)
    return q, k, v


def windowed_sink_gqa(u, uc, w_qkv, sink, w_o, cos, sin, need_ctx):
    bsz, n, _ = u.shape
    scale = HEAD_DIM ** -0.5
    q, k, v = split_qkv(u, w_qkv)
    q = (apply_rope(q, cos, sin) * scale).reshape(bsz, n, N_KV_HEADS, GROUP, HEAD_DIM)
    k = apply_rope(k, cos, sin)
    qc, kc, vc = split_qkv(uc, w_qkv)
    n_ctx = kc.shape[1]
    span = BLOCK + 2 * WINDOW
    kp = jnp.pad(k, ((0, 0), (WINDOW, WINDOW), (0, 0), (0, 0)))
    vp = jnp.pad(v, ((0, 0), (WINDOW, WINDOW), (0, 0), (0, 0)))
    sink_logit = sink.astype(jnp.float32).reshape(1, N_KV_HEADS, GROUP, 1, 1)

    def attend_block(start):
        qb = lax.dynamic_slice_in_dim(q, start, BLOCK, axis=1)
        kb = lax.dynamic_slice_in_dim(kp, start, span, axis=1)
        vb = lax.dynamic_slice_in_dim(vp, start, span, axis=1)
        qpos = start + jnp.arange(BLOCK)
        kpos = start - WINDOW + jnp.arange(span)
        valid = (jnp.abs(qpos[:, None] - kpos[None, :]) <= WINDOW) & (kpos >= 0) & (kpos < n)
        s_loc = jnp.einsum("bqhgd,bkhd->bhgqk", qb, kb).astype(jnp.float32)
        s_loc = jnp.where(valid, s_loc, NEG_INF)
        s_ctx = jnp.einsum("bqhgd,bchd->bhgqc", qb, kc).astype(jnp.float32)
        sinks = jnp.broadcast_to(sink_logit, s_ctx.shape[:-1] + (1,))
        p = jax.nn.softmax(jnp.concatenate([s_ctx, s_loc, sinks], axis=-1), axis=-1).astype(v.dtype)
        return (jnp.einsum("bhgqc,bchd->bqhgd", p[..., :n_ctx], vc)
                + jnp.einsum("bhgqk,bkhd->bqhgd", p[..., n_ctx:n_ctx + span], vb))

    o = lax.map(attend_block, jnp.arange(n // BLOCK) * BLOCK)
    y = jnp.moveaxis(o, 0, 1).reshape(bsz, n, N_HEADS * HEAD_DIM) @ w_o
    yc = None
    if need_ctx:
        qcg = (qc * scale).reshape(bsz, n_ctx, N_KV_HEADS, GROUP, HEAD_DIM)
        s = jnp.einsum("bqhgd,bchd->bhgqc", qcg, kc).astype(jnp.float32)
        sinks = jnp.broadcast_to(sink_logit, s.shape[:-1] + (1,))
        p = jax.nn.softmax(jnp.concatenate([s, sinks], axis=-1), axis=-1)[..., :n_ctx].astype(vc.dtype)
        yc = jnp.einsum("bhgqc,bchd->bqhgd", p, vc).reshape(bsz, n_ctx, N_HEADS * HEAD_DIM) @ w_o
    return y, yc


def rglru_coeffs(xr, gate_a_w, gate_a_b, gate_x_w, gate_x_b, lam, reset_pos):
    bsz, n, _ = xr.shape
    xb = xr.reshape(bsz, n, N_LRU_BLOCKS, LRU_BLOCK_W)
    r = jax.nn.sigmoid((jnp.einsum("bsnc,ncd->bsnd", xb, gate_a_w).reshape(bsz, n, D_RNN)
                        + gate_a_b).astype(jnp.float32))
    i = jax.nn.sigmoid((jnp.einsum("bsnc,ncd->bsnd", xb, gate_x_w).reshape(bsz, n, D_RNN)
                        + gate_x_b).astype(jnp.float32))
    log_a = -LRU_C * r * jax.nn.softplus(-lam.astype(jnp.float32))
    a = jnp.exp(log_a)
    mult = jnp.sqrt(-jnp.expm1(2.0 * log_a))
    if reset_pos is not None:
        mult = jnp.where(jnp.arange(n)[None, :, None] == reset_pos, 1.0, mult)
    return a, mult * i * xr.astype(jnp.float32)


def linear_scan(a, b, h0, reverse):
    if h0 is not None:
        edge = -1 if reverse else 0
        b = b.at[:, edge].add(a[:, edge] * h0)

    def combine(first, second):
        return first[0] * second[0], second[0] * first[1] + second[1]

    _, h = lax.associative_scan(combine, (a, b), reverse=reverse, axis=1)
    return h


def rglru_mixer(u, uc, w_in, conv_w, conv_b, gate_a_w, gate_a_b, gate_x_w, gate_x_b, lam, w_out,
                need_ctx):
    n_ctx = uc.shape[1]

    def branches(t):
        gate, rec = jnp.split(t @ w_in, 2, axis=-1)
        return jax.nn.gelu(gate), depthwise_conv(rec, conv_w, conv_b, LRU_CONV_PAD)

    g, xr = branches(u)
    gc, xrc = branches(uc)
    a_cf, b_cf = rglru_coeffs(xrc, gate_a_w[0], gate_a_b[0], gate_x_w[0], gate_x_b[0], lam[0], 0)
    a_cb, b_cb = rglru_coeffs(xrc, gate_a_w[1], gate_a_b[1], gate_x_w[1], gate_x_b[1], lam[1], n_ctx - 1)
    hc_f = linear_scan(a_cf, b_cf, None, False)
    hc_b = linear_scan(a_cb, b_cb, None, True)
    a_f, b_f = rglru_coeffs(xr, gate_a_w[0], gate_a_b[0], gate_x_w[0], gate_x_b[0], lam[0], None)
    a_b, b_b = rglru_coeffs(xr, gate_a_w[1], gate_a_b[1], gate_x_w[1], gate_x_b[1], lam[1], None)
    h_f = linear_scan(a_f, b_f, hc_f[:, -1], False)
    h_b = linear_scan(a_b, b_b, hc_b[:, 0], True)
    y = ((h_f + h_b).astype(u.dtype) * g) @ w_out
    yc = (((hc_f + hc_b).astype(uc.dtype) * gc) @ w_out) if need_ctx else None
    return y, yc


def conformer_conv(u, uc, w_in, b_in, dw_w, dw_b, ln_g, ln_b, w_out, b_out, need_ctx):
    def module(t):
        val, gt = jnp.split(t @ w_in + b_in, 2, axis=-1)
        z = val * jax.nn.sigmoid(gt)
        z = depthwise_conv(z, dw_w, dw_b, CONV_PAD)
        z = jax.nn.silu(layer_norm(z, ln_g, ln_b))
        return z @ w_out + b_out

    return module(u), (module(uc) if need_ctx else None)


def expert_choice_ffn(t, router, w1, w3, w2):
    bsz, n, dm = t.shape
    cap = max(1, EC_CAPACITY_FACTOR * n // N_EXPERTS)
    aff = jax.nn.softmax((t @ router).astype(jnp.float32), axis=-1)
    gate, idx = lax.top_k(jnp.swapaxes(aff, 1, 2), cap)
    xg = jax.vmap(lambda tb, ib: tb[ib])(t, idx)
    hid = jax.nn.silu(jnp.einsum("becd,edf->becf", xg, w1)) * jnp.einsum("becd,edf->becf", xg, w3)
    y = jnp.einsum("becf,efd->becd", hid, w2) * gate[..., None].astype(t.dtype)
    return jax.vmap(lambda yb, ib: jnp.zeros((n, dm), yb.dtype).at[ib.reshape(-1)].add(
        yb.reshape(-1, dm)))(y, idx)


def setup_inputs(seed: int = 0) -> dict:
    key = jax.random.key(seed)
    ks = iter(jax.random.split(key, 40))
    f32 = jnp.float32
    D, E, F = D_MODEL, N_EXPERTS, D_FF_EXPERT

    def nrm(shape, scale):
        return jax.random.normal(next(ks), shape, f32) * scale

    def gain(shape):
        return 1.0 + nrm(shape, 0.02)

    qkv_w = (N_HEADS + 2 * N_KV_HEADS) * HEAD_DIM
    a_c = jax.random.uniform(next(ks), (N_LRU_LAYERS, 2, D_RNN), f32, 0.9, 0.999)
    a_base = a_c ** (1.0 / LRU_C)
    lru_lambda = jnp.log(a_base) - jnp.log1p(-a_base)
    return {
        "x": nrm((BATCH, SEQ, D), 1.0),
        "c": nrm((BATCH, D), 1.0),
        "ctx": nrm((BATCH, CTX_LEN, D), 1.0),
        "c_ctx": nrm((D,), 1.0),
        "ada_w": nrm((DEPTH, D, 6 * D), 0.5 * D ** -0.5),
        "ada_b": nrm((DEPTH, 6 * D), 0.02),
        "ln1_g": gain((DEPTH, D)),
        "ln1_b": nrm((DEPTH, D), 0.02),
        "ln2_g": gain((DEPTH, D)),
        "ln2_b": nrm((DEPTH, D), 0.02),
        "attn_w_qkv": nrm((N_ATTN_LAYERS, D, qkv_w), D ** -0.5),
        "attn_sink": nrm((N_ATTN_LAYERS, N_HEADS), 1.0),
        "attn_w_o": nrm((N_ATTN_LAYERS, N_HEADS * HEAD_DIM, D), DEEPNORM_BETA * D ** -0.5),
        "lru_w_in": nrm((N_LRU_LAYERS, D, 2 * D_RNN), D ** -0.5),
        "lru_conv_w": nrm((N_LRU_LAYERS, LRU_CONV_W, D_RNN), LRU_CONV_W ** -0.5),
        "lru_conv_b": nrm((N_LRU_LAYERS, D_RNN), 0.02),
        "lru_gate_a_w": nrm((N_LRU_LAYERS, 2, N_LRU_BLOCKS, LRU_BLOCK_W, LRU_BLOCK_W), LRU_BLOCK_W ** -0.5),
        "lru_gate_a_b": nrm((N_LRU_LAYERS, 2, D_RNN), 0.02),
        "lru_gate_x_w": nrm((N_LRU_LAYERS, 2, N_LRU_BLOCKS, LRU_BLOCK_W, LRU_BLOCK_W), LRU_BLOCK_W ** -0.5),
        "lru_gate_x_b": nrm((N_LRU_LAYERS, 2, D_RNN), 0.02),
        "lru_lambda": lru_lambda,
        "lru_w_out": nrm((N_LRU_LAYERS, D_RNN, D), DEEPNORM_BETA * D_RNN ** -0.5),
        "conv_w_in": nrm((N_CONV_LAYERS, D, 2 * D), D ** -0.5),
        "conv_b_in": nrm((N_CONV_LAYERS, 2 * D), 0.02),
        "conv_dw_w": nrm((N_CONV_LAYERS, CONV_K, D), CONV_K ** -0.5),
        "conv_dw_b": nrm((N_CONV_LAYERS, D), 0.02),
        "conv_ln_g": gain((N_CONV_LAYERS, D)),
        "conv_ln_b": nrm((N_CONV_LAYERS, D), 0.02),
        "conv_w_out": nrm((N_CONV_LAYERS, D, D), DEEPNORM_BETA * D ** -0.5),
        "conv_b_out": nrm((N_CONV_LAYERS, D), 0.02),
        "moe_router": nrm((DEPTH, D, E), D ** -0.5),
        "moe_w1": nrm((DEPTH, E, D, F), D ** -0.5),
        "moe_w3": nrm((DEPTH, E, D, F), D ** -0.5),
        "moe_w2": nrm((DEPTH, E, F, D), DEEPNORM_BETA * F ** -0.5),
    }


def reference(x, c, ctx, c_ctx, ada_w, ada_b, ln1_g, ln1_b, ln2_g, ln2_b,
              attn_w_qkv, attn_sink, attn_w_o,
              lru_w_in, lru_conv_w, lru_conv_b, lru_gate_a_w, lru_gate_a_b, lru_gate_x_w,
              lru_gate_x_b, lru_lambda, lru_w_out,
              conv_w_in, conv_b_in, conv_dw_w, conv_dw_b, conv_ln_g, conv_ln_b, conv_w_out, conv_b_out,
              moe_router, moe_w1, moe_w3, moe_w2):
    cos, sin = axial_rope_tables(x.shape[1])
    h = ctx
    silu_c = jax.nn.silu(c)
    silu_cc = jax.nn.silu(c_ctx)
    for i in range(DEPTH):
        need_ctx = i < DEPTH - 1
        kind, j = i % N_MIXERS, i // N_MIXERS
        mod = silu_c @ ada_w[i] + ada_b[i]
        mod_c = silu_cc @ ada_w[i] + ada_b[i]
        sh1, sc1, g1, sh2, sc2, g2 = jnp.split(mod[:, None, :], 6, axis=-1)
        csh1, csc1, cg1, csh2, csc2, cg2 = jnp.split(mod_c, 6)
        u = x * (1 + sc1) + sh1
        uc = h * (1 + csc1) + csh1
        if kind == 0:
            y, yc = windowed_sink_gqa(u, uc, attn_w_qkv[j], attn_sink[j], attn_w_o[j], cos, sin, need_ctx)
        elif kind == 1:
            y, yc = rglru_mixer(u, uc, lru_w_in[j], lru_conv_w[j], lru_conv_b[j], lru_gate_a_w[j],
                                lru_gate_a_b[j], lru_gate_x_w[j], lru_gate_x_b[j], lru_lambda[j],
                                lru_w_out[j], need_ctx)
        else:
            y, yc = conformer_conv(u, uc, conv_w_in[j], conv_b_in[j], conv_dw_w[j], conv_dw_b[j],
                                   conv_ln_g[j], conv_ln_b[j], conv_w_out[j], conv_b_out[j], need_ctx)
        x = layer_norm(DEEPNORM_ALPHA * x + g1 * y, ln1_g[i], ln1_b[i])
        f = expert_choice_ffn(x * (1 + sc2) + sh2, moe_router[i], moe_w1[i], moe_w3[i], moe_w2[i])
        x = layer_norm(DEEPNORM_ALPHA * x + g2 * f, ln2_g[i], ln2_b[i])
        if need_ctx:
            h = layer_norm(DEEPNORM_ALPHA * h + cg1 * yc, ln1_g[i], ln1_b[i])
            fc = expert_choice_ffn(h * (1 + csc2) + csh2, moe_router[i], moe_w1[i], moe_w3[i], moe_w2[i])
            h = layer_norm(DEEPNORM_ALPHA * h + cg2 * fc, ln2_g[i], ln2_b[i])
    return x
```

```python
import functools

import jax
import jax.numpy as jnp
from jax import lax
from jax.experimental import pallas as pl
from jax.experimental.pallas import tpu as pltpu

F32 = jnp.float32
BF16 = jnp.bfloat16
I32 = jnp.int32

HEAD_DIM = 64
N_KV_HEADS = 4
GROUP = 4
WINDOW = 128
GRID_W = 64
ROPE_BASE = 10000.0
N_EXPERTS = 16
EC_CAPACITY_FACTOR = 2
LRU_C = 8.0
LN_EPS = 1e-5
NEG_INF = -1e30

ROW_TILE = 512
Q_TILE = 128
SEQ_CHUNK = 256
TOKEN_BLOCK = 256
SLOT_TILE = 128
FF_TILE = 512
LANES = 128
SUBLANES = 8
BF16_ROWS = 16
VMEM_LIMIT = 56 * 1024 * 1024


def _cparams(*sem):
    return pltpu.CompilerParams(dimension_semantics=sem, vmem_limit_bytes=VMEM_LIMIT)


def _layer_norm(y, g, b):
    mu = jnp.mean(y, axis=-1, keepdims=True)
    d = y - mu
    var = jnp.mean(d * d, axis=-1, keepdims=True)
    return d * lax.rsqrt(var + LN_EPS) * g + b


def _dot(a, b):
    return jnp.dot(a, b, preferred_element_type=F32)


def _dot_nt(a, b):
    return lax.dot_general(a, b, (((1,), (1,)), ((), ())), preferred_element_type=F32)


def _dot_tn(a, b):
    return lax.dot_general(a, b, (((0,), (0,)), ((), ())), preferred_element_type=F32)


def _ada_kernel(c_ref, w_ref, b_ref, o_ref):
    cnd = c_ref[...]
    s = cnd * jax.nn.sigmoid(cnd)
    o_ref[0] = _dot(s.astype(BF16), w_ref[0].astype(BF16)) + b_ref[0]


def _ada_call(cond, ada_w, ada_b):
    depth, d, n6 = ada_w.shape
    rows = cond.shape[0]
    tn = 1536
    return pl.pallas_call(
        _ada_kernel,
        out_shape=jax.ShapeDtypeStruct((depth, rows, n6), F32),
        grid=(depth, n6 // tn),
        in_specs=[
            pl.BlockSpec((rows, d), lambda l, n: (0, 0)),
            pl.BlockSpec((1, d, tn), lambda l, n: (l, 0, n)),
            pl.BlockSpec((1, 1, tn), lambda l, n: (l, 0, n)),
        ],
        out_specs=pl.BlockSpec((1, rows, tn), lambda l, n: (l, 0, n)),
        compiler_params=_cparams("arbitrary", "arbitrary"),
        name="ada_mod",
    )(cond, ada_w, ada_b.reshape(depth, 1, n6))


def _modulate1(x_ref, mod_ref):
    m = mod_ref[0]
    return (x_ref[...] * (1.0 + m[1:2]) + m[0:1]).astype(BF16)


def _attn_in_kernel(x_ref, mod_ref, w_ref, cos_ref, sin_ref, q_ref, k_ref, v_ref):
    d = x_ref.shape[1]
    z = _dot(_modulate1(x_ref, mod_ref), w_ref[...])
    cos = cos_ref[...]
    sin = sin_ref[...]
    lane = lax.broadcasted_iota(I32, cos.shape, 1)
    first_half = (lane % HEAD_DIM) < (HEAD_DIM // 2)

    def rope(t):
        swapped = jnp.where(first_half, pltpu.roll(t, LANES - HEAD_DIM // 2, 1),
                            pltpu.roll(t, HEAD_DIM // 2, 1))
        return t * cos + swapped * sin

    scale = HEAD_DIM ** -0.5
    for cidx in range(d // LANES):
        sl = slice(cidx * LANES, (cidx + 1) * LANES)
        q_ref[:, sl] = (rope(z[:, sl]) * scale).astype(BF16)
    kvw = k_ref.shape[1]
    for cidx in range(kvw // LANES):
        sl = slice(cidx * LANES, (cidx + 1) * LANES)
        k_ref[:, sl] = rope(z[:, d + cidx * LANES:d + (cidx + 1) * LANES]).astype(BF16)
    v_ref[...] = z[:, d + kvw:].astype(BF16)


def _lru_in_kernel(x_ref, mod_ref, w_ref, g_ref, rec_ref):
    d = x_ref.shape[1]
    z = _dot(_modulate1(x_ref, mod_ref), w_ref[...])
    g_ref[...] = jax.nn.gelu(z[:, :d])
    rec_ref[...] = z[:, d:]


def _conv_in_kernel(x_ref, mod_ref, w_ref, b_ref, o_ref):
    d = x_ref.shape[1]
    z = _dot(_modulate1(x_ref, mod_ref), w_ref[...]) + b_ref[...]
    o_ref[...] = z[:, :d] * jax.nn.sigmoid(z[:, d:])


def _row_specs(tm, d, tiles_per_sample, n_samples):
    x_spec = pl.BlockSpec((tm, d), lambda i: (i, 0))
    mod_spec = pl.BlockSpec((1, 6, d), lambda i: (jnp.minimum(i // tiles_per_sample, n_samples), 0, 0))
    return x_spec, mod_spec


def _full(shape):
    n = len(shape)
    return pl.BlockSpec(shape, lambda *_: (0,) * n)


def _attn_kernel(seq, n_lat_steps, q_ref, kp_ref, kc_ref, kn_ref, vp_ref, vc_ref, vn_ref,
                 kx_ref, vx_ref, sink_ref, o_ref):
    i = pl.program_id(1)
    tq = q_ref.shape[0]
    n_ctx = kx_ref.shape[0]
    span = tq + 2 * WINDOW
    qpos = i * tq + lax.broadcasted_iota(I32, (tq, span), 0)
    kpos = i * tq - WINDOW + lax.broadcasted_iota(I32, (tq, span), 1)
    seq_eff = jnp.where(i < n_lat_steps, seq, 0)
    valid = (jnp.abs(qpos - kpos) <= WINDOW) & (kpos >= 0) & (kpos < seq_eff)
    bias = jnp.concatenate(
        [jnp.zeros((tq, n_ctx), F32), jnp.where(valid, 0.0, NEG_INF).astype(F32)], axis=1)
    for h in range(N_KV_HEADS):
        hs = slice(h * HEAD_DIM, (h + 1) * HEAD_DIM)
        kk = jnp.concatenate([kx_ref[:, hs], kp_ref[:, hs], kc_ref[:, hs], kn_ref[:, hs]], axis=0)
        vv = jnp.concatenate([vx_ref[:, hs], vp_ref[:, hs], vc_ref[:, hs], vn_ref[:, hs]], axis=0)
        for g in range(GROUP):
            hq = h * GROUP + g
            qs = slice(hq * HEAD_DIM, (hq + 1) * HEAD_DIM)
            s = _dot_nt(q_ref[:, qs], kk) + bias
            sink = sink_ref[0:1, hq:hq + 1]
            m = jnp.maximum(jnp.max(s, axis=-1, keepdims=True), sink)
            e = jnp.exp(s - m)
            den = jnp.sum(e, axis=-1, keepdims=True) + jnp.exp(sink - m)
            o = _dot(e.astype(BF16), vv) / den
            o_ref[:, qs] = o.astype(BF16)


def _attn_call(q, k, v, sink, batch, seq, n_ctx):
    rows, d = q.shape
    kvw = k.shape[1]
    tq = Q_TILE
    assert seq % tq == 0 and n_ctx % tq == 0 and tq % WINDOW == 0 and (batch * seq) % n_ctx == 0
    nl, nc = seq // tq, n_ctx // tq
    wb = seq // WINDOW
    r = tq // WINDOW

    def q_map(b, i):
        return (jnp.where(i < nl, b * nl + i, batch * nl + b * nc + (i - nl)), 0)

    def prev_map(b, i):
        return (b * wb + jnp.clip(i * r - 1, 0, wb - 1), 0)

    def cur_map(b, i):
        return (b * nl + jnp.minimum(i, nl - 1), 0)

    def next_map(b, i):
        return (b * wb + jnp.minimum((i + 1) * r, wb - 1), 0)

    def ctx_map(b, i):
        return ((batch * seq) // n_ctx + b, 0)

    win = lambda m: pl.BlockSpec((WINDOW, kvw), m)
    cur = pl.BlockSpec((tq, kvw), cur_map)
    ctxs = pl.BlockSpec((n_ctx, kvw), ctx_map)
    return pl.pallas_call(
        functools.partial(_attn_kernel, seq, nl),
        out_shape=jax.ShapeDtypeStruct((rows, d), BF16),
        grid=(batch, nl + nc),
        in_specs=[pl.BlockSpec((tq, d), q_map),
                  win(prev_map), cur, win(next_map), win(prev_map), cur, win(next_map),
                  ctxs, ctxs, _full(sink.shape)],
        out_specs=pl.BlockSpec((tq, d), q_map),
        compiler_params=_cparams("arbitrary", "arbitrary"),
        name="window_attn",
    )(q, k, k, k, v, v, v, k, v, sink)


def _shift_rows(x, k):
    return pltpu.roll(x, k % x.shape[0], 0)


def _lru_kernel(n_ctx_chunks, n_lat_chunks,
                pf_ref, cf_ref, nf_ref, pb_ref, cb_ref, nb_ref,
                cw_ref, cbias_ref, wa_ref, ba_ref, wx_ref, bx_ref, lam_ref,
                hf_ref, hb_ref, carry_ref):
    j = pl.program_id(1)
    tc, d = cf_ref.shape
    halo = pf_ref.shape[0]
    nblk = wa_ref.shape[1]
    bw = d // nblk
    is_ctx = j < n_ctx_chunks
    jf = jnp.where(is_ctx, j, j - n_ctx_chunks)
    nseq = jnp.where(is_ctx, n_ctx_chunks, n_lat_chunks)
    jb = nseq - 1 - jf

    @pl.when(j == 0)
    def _():
        carry_ref[...] = jnp.zeros(carry_ref.shape, F32)

    sub = lax.broadcasted_iota(I32, (tc, d), 0) % SUBLANES
    row_blk = lax.broadcasted_iota(I32, (tc, bw), 0)

    def coeffs(direction, p_ref, c_ref, n_ref, jj):
        prev = p_ref[...] * (jj > 0).astype(F32)
        nxt = n_ref[...] * (jj < nseq - 1).astype(F32)
        ext = jnp.concatenate([prev, c_ref[...], nxt], axis=0)
        xr = jnp.zeros((tc, d), F32) + cbias_ref[...]
        for k in range(cw_ref.shape[0]):
            xr = xr + cw_ref[k:k + 1, :] * _shift_rows(ext, 2 - k)[halo:halo + tc]
        a_parts, b_parts = [], []
        for n in range(nblk):
            cs = slice(n * bw, (n + 1) * bw)
            xb = xr[:, cs].astype(BF16)
            r = jax.nn.sigmoid(_dot(xb, wa_ref[direction, n]) + ba_ref[direction:direction + 1, cs])
            gi = jax.nn.sigmoid(_dot(xb, wx_ref[direction, n]) + bx_ref[direction:direction + 1, cs])
            log_a = -LRU_C * r * jax.nn.softplus(-lam_ref[direction:direction + 1, cs])
            a = jnp.exp(log_a)
            mult = jnp.sqrt(1.0 - a * a)
            at_edge = is_ctx & (jj == (0 if direction == 0 else nseq - 1))
            reset_row = jnp.where(at_edge, 0 if direction == 0 else tc - 1, -1)
            mult = jnp.where(row_blk == reset_row, 1.0, mult)
            a_parts.append(a)
            b_parts.append(mult * gi * xr[:, cs])
        return jnp.concatenate(a_parts, axis=1), jnp.concatenate(b_parts, axis=1)

    def scan(direction, a, bb, out_ref):
        for k in (1, 2, 4):
            if direction == 0:
                keep = sub >= k
                a_sh, b_sh = _shift_rows(a, k), _shift_rows(bb, k)
            else:
                keep = sub < SUBLANES - k
                a_sh, b_sh = _shift_rows(a, -k), _shift_rows(bb, -k)
            bb = jnp.where(keep, a * b_sh + bb, bb)
            a = jnp.where(keep, a * a_sh, a)
        carry = carry_ref[direction:direction + 1, :]
        ngroups = tc // SUBLANES
        order = range(ngroups) if direction == 0 else range(ngroups - 1, -1, -1)
        for gidx in order:
            rs = slice(gidx * SUBLANES, (gidx + 1) * SUBLANES)
            h = a[rs] * carry + bb[rs]
            out_ref[rs, :] = h
            carry = h[SUBLANES - 1:SUBLANES] if direction == 0 else h[0:1]
        carry_ref[direction:direction + 1, :] = carry

    a_f, b_f = coeffs(0, pf_ref, cf_ref, nf_ref, jf)
    scan(0, a_f, b_f, hf_ref)
    a_b, b_b = coeffs(1, pb_ref, cb_ref, nb_ref, jb)
    scan(1, a_b, b_b, hb_ref)


def _lru_call(rec, conv_w, conv_b, wa, ba, wx, bx, lam, batch, seq, n_ctx):
    rows, d = rec.shape
    tc = SEQ_CHUNK
    halo = SUBLANES
    assert seq % tc == 0 and n_ctx % tc == 0
    ncc, nlc = n_ctx // tc, seq // tc
    hb_per_chunk = tc // halo
    last_halo = rows // halo - 1

    def chunk_f(b, j):
        return jnp.where(j < ncc, (batch * seq) // tc + b * ncc + j, b * nlc + (j - ncc))

    def chunk_b(b, j):
        return jnp.where(j < ncc, (batch * seq) // tc + b * ncc + (ncc - 1 - j),
                         b * nlc + (nlc - 1 - (j - ncc)))

    def cur(f):
        return pl.BlockSpec((tc, d), lambda b, j: (f(b, j), 0))

    def prev(f):
        return pl.BlockSpec((halo, d), lambda b, j: (jnp.maximum(f(b, j) * hb_per_chunk - 1, 0), 0))

    def nxt(f):
        return pl.BlockSpec((halo, d),
                            lambda b, j: (jnp.minimum((f(b, j) + 1) * hb_per_chunk, last_halo), 0))

    out = jax.ShapeDtypeStruct((rows, d), F32)
    return pl.pallas_call(
        functools.partial(_lru_kernel, ncc, nlc),
        out_shape=(out, out),
        grid=(batch, ncc + nlc),
        in_specs=[prev(chunk_f), cur(chunk_f), nxt(chunk_f), prev(chunk_b), cur(chunk_b), nxt(chunk_b),
                  _full(conv_w.shape), _full(conv_b.shape), _full(wa.shape), _full(ba.shape),
                  _full(wx.shape), _full(bx.shape), _full(lam.shape)],
        out_specs=(cur(chunk_f), cur(chunk_b)),
        scratch_shapes=[pltpu.VMEM((2, d), F32)],
        compiler_params=_cparams("arbitrary", "arbitrary"),
        name="rglru_scan",
    )(rec, rec, rec, rec, rec, rec, conv_w, conv_b, wa, ba, wx, bx, lam)


def _dwconv_kernel(n_lat_chunks_total, lat_chunks, ctx_chunks,
                   p_ref, c_ref, n_ref, w_ref, b_ref, g_ref, beta_ref, o_ref):
    ci = pl.program_id(0)
    tc, d = c_ref.shape
    halo = p_ref.shape[0]
    ktaps = w_ref.shape[0]
    is_lat = ci < n_lat_chunks_total
    pos = jnp.where(is_lat, ci % lat_chunks, (ci - n_lat_chunks_total) % ctx_chunks)
    nseq = jnp.where(is_lat, lat_chunks, ctx_chunks)
    prev = p_ref[...] * (pos > 0).astype(F32)
    nxt = n_ref[...] * (pos < nseq - 1).astype(F32)
    ext = jnp.concatenate([prev, c_ref[...], nxt], axis=0)
    acc = jnp.zeros((tc, d), F32) + b_ref[...]
    for k in range(ktaps):
        acc = acc + w_ref[k:k + 1, :] * _shift_rows(ext, ktaps // 2 - k)[halo:halo + tc]
    z = _layer_norm(acc, g_ref[...], beta_ref[...])
    o_ref[...] = (z * jax.nn.sigmoid(z)).astype(BF16)


def _dwconv_call(glu, w, b, g, beta, batch, seq, n_ctx):
    rows, d = glu.shape
    tc = SEQ_CHUNK
    halo = 2 * SUBLANES
    assert w.shape[0] // 2 <= halo and seq % tc == 0 and n_ctx % tc == 0
    per = tc // halo
    last_halo = rows // halo - 1
    return pl.pallas_call(
        functools.partial(_dwconv_kernel, batch * seq // tc, seq // tc, n_ctx // tc),
        out_shape=jax.ShapeDtypeStruct((rows, d), BF16),
        grid=(rows // tc,),
        in_specs=[pl.BlockSpec((halo, d), lambda i: (jnp.maximum(i * per - 1, 0), 0)),
                  pl.BlockSpec((tc, d), lambda i: (i, 0)),
                  pl.BlockSpec((halo, d), lambda i: (jnp.minimum((i + 1) * per, last_halo), 0)),
                  _full(w.shape), _full(b.shape), _full(g.shape), _full(beta.shape)],
        out_specs=pl.BlockSpec((tc, d), lambda i: (i, 0)),
        compiler_params=_cparams("arbitrary"),
        name="dwconv_ln_swish",
    )(glu, glu, glu, w, b, g, beta)


def _post_kernel(kind, has_bias, alpha, *refs):
    refs = list(refs)
    if kind == "lru":
        hf_ref, hb_ref, gate_ref = refs[:3]
        refs = refs[3:]
        core = ((hf_ref[...] + hb_ref[...]) * gate_ref[...]).astype(BF16)
    else:
        core = refs[0][...]
        refs = refs[1:]
    w_ref = refs[0]
    refs = refs[1:]
    y = _dot(core, w_ref[...])
    if has_bias:
        y = y + refs[0][...]
        refs = refs[1:]
    x_ref, mod_ref, lng_ref, lnb_ref, rt_ref, x1_ref, t_ref, aff_ref = refs
    m = mod_ref[0]
    x1 = _layer_norm(alpha * x_ref[...] + m[2:3] * y, lng_ref[...], lnb_ref[...])
    x1_ref[...] = x1
    t = (x1 * (1.0 + m[4:5]) + m[3:4]).astype(BF16)
    t_ref[...] = t
    logits = _dot_nt(rt_ref[...], t)
    e = jnp.exp(logits - jnp.max(logits, axis=0, keepdims=True))
    aff_ref[...] = e / jnp.sum(e, axis=0, keepdims=True)


def _post_call(kind, alpha, cores, w, bias, x_all, mod, ln_g, ln_b, router_t, n_rows, tiles_per_sample,
               n_samples):
    d = x_all.shape[1]
    tm = ROW_TILE
    n_e = router_t.shape[0]
    x_spec, mod_spec = _row_specs(tm, d, tiles_per_sample, n_samples)
    in_specs = [x_spec] * len(cores) + [_full(w.shape)]
    args = list(cores) + [w]
    if bias is not None:
        in_specs.append(_full(bias.shape))
        args.append(bias)
    in_specs += [x_spec, mod_spec, _full(ln_g.shape), _full(ln_b.shape), _full(router_t.shape)]
    args += [x_all, mod, ln_g, ln_b, router_t]
    return pl.pallas_call(
        functools.partial(_post_kernel, kind, bias is not None, alpha),
        out_shape=(jax.ShapeDtypeStruct((n_rows, d), F32),
                   jax.ShapeDtypeStruct((n_rows, d), BF16),
                   jax.ShapeDtypeStruct((n_e, n_rows), F32)),
        grid=(n_rows // tm,),
        in_specs=in_specs,
        out_specs=(x_spec, x_spec, pl.BlockSpec((n_e, tm), lambda i: (0, i))),
        compiler_params=_cparams("arbitrary"),
        name="mixer_out_ln_router",
    )(*args)


def _route_kernel(cap, slot_tile, aff_ref, pos_ref, excl_ref, lo_ref, hi_ref):
    n_e, seq = aff_ref.shape
    tb = min(TOKEN_BLOCK, seq)
    nblk = seq // tb
    bits = pltpu.bitcast(aff_ref[...], I32)

    def search(_, lohi):
        lo, hi = lohi
        mid = lo + jnp.right_shift(hi - lo, 1)
        cnt = jnp.sum((bits >= mid).astype(F32), axis=1, keepdims=True)
        ge = cnt >= cap
        return jnp.where(ge, mid, lo), jnp.where(ge, hi, mid)

    lo0 = jnp.zeros((n_e, 1), I32)
    hi0 = jnp.full((n_e, 1), 0x7F800000, I32)
    thr, _ = lax.fori_loop(0, 31, search, (lo0, hi0))
    n_gt = jnp.sum((bits > thr).astype(F32), axis=1, keepdims=True)
    need = cap - n_gt

    upper = (lax.broadcasted_iota(I32, (tb, tb), 0) < lax.broadcasted_iota(I32, (tb, tb), 1)).astype(BF16)
    lane = lax.broadcasted_iota(I32, (n_e, LANES), 1)
    eq_off = jnp.zeros((n_e, 1), F32)
    pos_off = jnp.zeros((n_e, 1), F32)
    excl = jnp.zeros((n_e, LANES), F32)
    incl = jnp.full((n_e, LANES), 1e9, F32)
    for blk in range(nblk):
        cs = slice(blk * tb, (blk + 1) * tb)
        bb = bits[:, cs]
        gt = bb > thr
        eq = bb == thr
        eqf = jnp.where(eq, 1.0, 0.0)
        rank = _dot(eqf.astype(BF16), upper) + eq_off
        sel = gt | (eq & (rank < need))
        self_f = jnp.where(sel, 1.0, 0.0)
        pin = _dot(self_f.astype(BF16), upper) + pos_off
        pos_ref[:, cs] = jnp.where(sel, pin, -1.0)
        excl = jnp.where(lane == blk, pos_off, excl)
        eq_off = eq_off + jnp.sum(eqf, axis=1, keepdims=True)
        pos_off = pos_off + jnp.sum(self_f, axis=1, keepdims=True)
        incl = jnp.where(lane == blk, pos_off, incl)
    excl_ref[0] = excl.astype(I32)
    lo = jnp.zeros((n_e, LANES), F32)
    hi = jnp.zeros((n_e, LANES), F32)
    for q in range(cap // slot_tile):
        first = float(q * slot_tile)
        last = float(q * slot_tile + slot_tile - 1)
        lo_q = jnp.sum(jnp.where(incl <= first, 1.0, 0.0), axis=1, keepdims=True)
        hi_q = jnp.sum(jnp.where(incl <= last, 1.0, 0.0), axis=1, keepdims=True)
        lo = jnp.where(lane == q, lo_q, lo)
        hi = jnp.where(lane == q, jnp.minimum(hi_q, float(nblk - 1)), hi)
    lo_ref[0] = lo.astype(I32)
    hi_ref[0] = hi.astype(I32)


def _route_call(aff_t, batch, seq, first_block, cap, slot_tile):
    n_e = aff_t.shape[0]
    assert seq // min(TOKEN_BLOCK, seq) <= LANES and cap // slot_tile <= LANES
    meta = jax.ShapeDtypeStruct((batch, n_e, LANES), I32)
    meta_spec = pl.BlockSpec((1, n_e, LANES), lambda b: (b, 0, 0))
    return pl.pallas_call(
        functools.partial(_route_kernel, cap, slot_tile),
        out_shape=(jax.ShapeDtypeStruct((n_e, batch * seq), F32), meta, meta, meta),
        grid=(batch,),
        in_specs=[pl.BlockSpec((n_e, seq), lambda b: (0, first_block + b))],
        out_specs=(pl.BlockSpec((n_e, seq), lambda b: (0, b)), meta_spec, meta_spec, meta_spec),
        compiler_params=_cparams("arbitrary"),
        name="expert_choice_route",
    )(aff_t)


def _dispatch_kernel(slot_tile, lo_ref, hi_ref, t_ref, pos_ref, aff_ref, xg_ref, gate_ref):
    b = pl.program_id(0)
    e = pl.program_id(1)
    n_e = pl.num_programs(1)
    cap, d = xg_ref.shape[2], xg_ref.shape[3]
    tb = pos_ref.shape[3]
    nq = cap // slot_tile
    slot = lax.broadcasted_iota(I32, (slot_tile, tb), 0).astype(F32)
    for q in range(nq):
        base = (b * n_e + e) * nq + q
        lo = lo_ref[base]
        hi = hi_ref[base]

        def body(blk, carry):
            acc, gacc = carry
            pos = pos_ref[0, blk]
            hit = (pos - float(q * slot_tile)) == slot
            x_blk = t_ref[pl.ds(pl.multiple_of(blk * tb, tb), tb), :]
            acc = acc + _dot(jnp.where(hit, 1.0, 0.0).astype(BF16), x_blk)
            gacc = gacc + jnp.sum(jnp.where(hit, aff_ref[0, blk], 0.0), axis=1, keepdims=True)
            return acc, gacc

        acc, gacc = lax.fori_loop(lo, hi + 1, body,
                                  (jnp.zeros((slot_tile, d), F32), jnp.zeros((slot_tile, 1), F32)))
        xg_ref[0, 0, q * slot_tile:(q + 1) * slot_tile, :] = acc.astype(BF16)
        gate_ref[0, 0, q * slot_tile:(q + 1) * slot_tile, :] = gacc


def _dispatch_call(t_all, pos, aff_t, lo, hi, batch, seq, first_seq_block, aff_first_block, cap, slot_tile):
    d = t_all.shape[1]
    n_e = pos.shape[0]
    tb = min(TOKEN_BLOCK, seq)
    nblk = seq // tb
    pos4 = pos.reshape(n_e, batch * nblk, 1, tb)
    aff4 = aff_t.reshape(n_e, aff_t.shape[1] // tb, 1, tb)
    grid_spec = pltpu.PrefetchScalarGridSpec(
        num_scalar_prefetch=2,
        grid=(batch, n_e),
        in_specs=[
            pl.BlockSpec((seq, d), lambda b, e, *_: (first_seq_block + b, 0), pipeline_mode=pl.Buffered(1)),
            pl.BlockSpec((1, nblk, 1, tb), lambda b, e, *_: (e, b, 0, 0)),
            pl.BlockSpec((1, nblk, 1, tb), lambda b, e, *_: (e, aff_first_block + b, 0, 0)),
        ],
        out_specs=(pl.BlockSpec((1, 1, cap, d), lambda b, e, *_: (b, e, 0, 0)),
                   pl.BlockSpec((1, 1, cap, 1), lambda b, e, *_: (b, e, 0, 0))),
    )
    return pl.pallas_call(
        functools.partial(_dispatch_kernel, slot_tile),
        out_shape=(jax.ShapeDtypeStruct((batch, n_e, cap, d), BF16),
                   jax.ShapeDtypeStruct((batch, n_e, cap, 1), F32)),
        grid_spec=grid_spec,
        compiler_params=_cparams("arbitrary", "arbitrary"),
        name="moe_dispatch",
    )(lo.reshape(-1), hi.reshape(-1), t_all, pos4, aff4)


def _ffn_kernel(xg_ref, w1_ref, w3_ref, w2_ref, gate_ref, y_ref, acc_ref):
    f = pl.program_id(2)
    xg = xg_ref[0, 0]
    h1 = _dot(xg, w1_ref[0])
    h3 = _dot(xg, w3_ref[0])
    hid = (h1 * jax.nn.sigmoid(h1) * h3).astype(BF16)
    part = _dot(hid, w2_ref[0])

    @pl.when(f == 0)
    def _():
        acc_ref[...] = part

    @pl.when(f > 0)
    def _():
        acc_ref[...] += part

    @pl.when(f == pl.num_programs(2) - 1)
    def _():
        y_ref[0, 0] = (acc_ref[...] * gate_ref[0, 0]).astype(BF16)


def _ffn_call(xg, gate, w1, w3, w2):
    batch, n_e, cap, d = xg.shape
    ff = w1.shape[2]
    tf = FF_TILE
    return pl.pallas_call(
        _ffn_kernel,
        out_shape=jax.ShapeDtypeStruct((batch, n_e, cap, d), BF16),
        grid=(n_e, batch, ff // tf),
        in_specs=[pl.BlockSpec((1, 1, cap, d), lambda e, b, f: (b, e, 0, 0)),
                  pl.BlockSpec((1, d, tf), lambda e, b, f: (e, 0, f)),
                  pl.BlockSpec((1, d, tf), lambda e, b, f: (e, 0, f)),
                  pl.BlockSpec((1, tf, d), lambda e, b, f: (e, f, 0)),
                  pl.BlockSpec((1, 1, cap, 1), lambda e, b, f: (b, e, 0, 0))],
        out_specs=pl.BlockSpec((1, 1, cap, d), lambda e, b, f: (b, e, 0, 0)),
        scratch_shapes=[pltpu.VMEM((cap, d), F32)],
        compiler_params=_cparams("arbitrary", "arbitrary", "arbitrary"),
        name="expert_ffn",
    )(xg, w1, w3, w2, gate)


def _combine_kernel(n_e, chunk_rows, nblk, has_alias, alpha, excl_ref, *refs):
    if has_alias:
        refs = refs[1:]
    x1_ref, pos_ref, mod_ref, lng_ref, lnb_ref, y_hbm, o_ref, buf_ref, sem_ref = refs
    b = pl.program_id(0)
    i = pl.program_id(1)
    tb, d = x1_ref.shape
    cap = y_hbm.shape[2]

    def chunk_start(e):
        first = excl_ref[(b * n_e + e) * LANES + i]
        aligned = (first // BF16_ROWS) * BF16_ROWS
        return pl.multiple_of(jnp.minimum(aligned, cap - chunk_rows), BF16_ROWS)

    def copy(e, start):
        return pltpu.make_async_copy(y_hbm.at[b, e, pl.ds(start, chunk_rows)], buf_ref.at[e], sem_ref.at[e])

    @pl.when(i < nblk)
    def _():
        starts = [chunk_start(e) for e in range(n_e)]
        for e in range(n_e):
            copy(e, starts[e]).start()
        row = lax.broadcasted_iota(I32, (chunk_rows, tb), 0).astype(F32)
        f = jnp.zeros((tb, d), F32)
        for e in range(n_e):
            copy(e, starts[e]).wait()
            hit = (pos_ref[e:e + 1, :] - starts[e].astype(F32)) == row
            f = f + _dot_tn(jnp.where(hit, 1.0, 0.0).astype(BF16), buf_ref[e])
        m = mod_ref[0]
        o_ref[...] = _layer_norm(alpha * x1_ref[...] + m[5:6] * f, lng_ref[...], lnb_ref[...])

    @pl.when(i >= nblk)
    def _():
        o_ref[...] = x1_ref[...]


def _combine_call(alpha, x1, pos, excl, y, mod, ln_g, ln_b, batch, seq, x_first_block, mod_row, out_rows,
                  alias_buf, carry_first_block=0, carry_blocks=0):
    d = x1.shape[1]
    n_e, cap = y.shape[1], y.shape[2]
    tb = min(TOKEN_BLOCK, seq)
    nblk = seq // tb
    chunk_rows = min(tb + BF16_ROWS, cap)
    has_alias = alias_buf is not None

    def row_map(b, i, *_):
        return (jnp.where(i < nblk, x_first_block + b * nblk + i,
                          carry_first_block + b * carry_blocks + (i - nblk)), 0)

    mod_map = (lambda b, i, *_: (b, 0, 0)) if mod_row is None else (lambda b, i, *_: (mod_row, 0, 0))
    in_specs = [pl.BlockSpec((tb, d), row_map),
                pl.BlockSpec((n_e, tb), lambda b, i, *_: (0, b * nblk + jnp.minimum(i, nblk - 1))),
                pl.BlockSpec((1, 6, d), mod_map),
                _full(ln_g.shape), _full(ln_b.shape),
                pl.BlockSpec(memory_space=pl.ANY)]
    args = [x1, pos, mod, ln_g, ln_b, y]
    aliases = {}
    if has_alias:
        in_specs = [pl.BlockSpec(memory_space=pl.ANY)] + in_specs
        args = [alias_buf] + args
        aliases = {1: 0}
    grid_spec = pltpu.PrefetchScalarGridSpec(
        num_scalar_prefetch=1,
        grid=(batch, nblk + carry_blocks),
        in_specs=in_specs,
        out_specs=pl.BlockSpec((tb, d), row_map),
        scratch_shapes=[pltpu.VMEM((n_e, chunk_rows, d), BF16), pltpu.SemaphoreType.DMA((n_e,))],
    )
    return pl.pallas_call(
        functools.partial(_combine_kernel, n_e, chunk_rows, nblk, has_alias, alpha),
        out_shape=jax.ShapeDtypeStruct((out_rows, d), F32),
        grid_spec=grid_spec,
        input_output_aliases=aliases,
        compiler_params=_cparams("arbitrary", "arbitrary"),
        name="moe_combine_ln",
    )(excl.reshape(-1), *args)


def _rope_tables(seq, pad_rows):
    rows = seq // GRID_W
    row = jnp.broadcast_to(jnp.arange(rows, dtype=F32)[:, None], (rows, GRID_W)).reshape(-1)
    col = jnp.broadcast_to(jnp.arange(GRID_W, dtype=F32)[None, :], (rows, GRID_W)).reshape(-1)
    n_freq = HEAD_DIM // 4
    inv = ROPE_BASE ** (-jnp.arange(n_freq, dtype=F32) / n_freq)
    ang = jnp.concatenate([row[:, None] * inv, col[:, None] * inv], axis=-1)
    cos, sin = jnp.cos(ang), jnp.sin(ang)
    reps = LANES // HEAD_DIM
    cos_t = jnp.tile(jnp.concatenate([cos, cos], axis=-1), (1, reps))
    sin_t = jnp.tile(jnp.concatenate([-sin, sin], axis=-1), (1, reps))
    cos_t = jnp.concatenate([cos_t, jnp.ones((pad_rows, LANES), F32)], axis=0)
    sin_t = jnp.concatenate([sin_t, jnp.zeros((pad_rows, LANES), F32)], axis=0)
    return cos_t, sin_t


def _moe(alpha, x1, t, aff_t, mod, ln_g, ln_b, w1, w3, w2, batch, seq, n_ctx, with_ctx, last):
    n_e = aff_t.shape[0]
    n_lat = batch * seq
    cap = max(1, EC_CAPACITY_FACTOR * seq // n_e)
    st = min(SLOT_TILE, cap)
    pos, excl, lo, hi = _route_call(aff_t, batch, seq, 0, cap, st)
    xg, gate = _dispatch_call(t, pos, aff_t, lo[:, :, :cap // st], hi[:, :, :cap // st],
                              batch, seq, 0, 0, cap, st)
    y = _ffn_call(xg, gate, w1, w3, w2)
    out_rows = n_lat if last else x1.shape[0]
    tb = min(TOKEN_BLOCK, seq)
    assert last or n_ctx % tb == 0
    x2 = _combine_call(alpha, x1, pos, excl, y, mod, ln_g, ln_b, batch, seq, 0, None, out_rows, None,
                       carry_first_block=n_lat // tb, carry_blocks=0 if last else n_ctx // tb)
    if with_ctx:
        capc = max(1, EC_CAPACITY_FACTOR * n_ctx // n_e)
        stc = min(SLOT_TILE, capc)
        tbc = min(TOKEN_BLOCK, n_ctx)
        posc, exclc, loc, hic = _route_call(aff_t, batch, n_ctx, n_lat // n_ctx, capc, stc)
        xgc, gatec = _dispatch_call(t, posc, aff_t, loc[:, :, :capc // stc], hic[:, :, :capc // stc],
                                    batch, n_ctx, n_lat // n_ctx, n_lat // tbc, capc, stc)
        yc = _ffn_call(xgc, gatec, w1, w3, w2)
        x2 = _combine_call(alpha, x1, posc, exclc, yc, mod, ln_g, ln_b, batch, n_ctx, n_lat // tbc, batch,
                           x1.shape[0], x2)
    return x2


def kernel(x, c, ctx, c_ctx, ada_w, ada_b, ln1_g, ln1_b, ln2_g, ln2_b, attn_w_qkv, attn_sink, attn_w_o, lru_w_in, lru_conv_w, lru_conv_b, lru_gate_a_w, lru_gate_a_b, lru_gate_x_w, lru_gate_x_b, lru_lambda, lru_w_out, conv_w_in, conv_b_in, conv_dw_w, conv_dw_b, conv_ln_g, conv_ln_b, conv_w_out, conv_b_out, moe_router, moe_w1, moe_w3, moe_w2):
    batch, seq, d = x.shape
    n_ctx = ctx.shape[1]
    depth = ada_w.shape[0]
    n_lat = batch * seq
    rows = n_lat + batch * n_ctx
    tm = ROW_TILE
    assert seq % tm == 0 and (batch * n_ctx) % tm == 0
    tiles_per_sample = seq // tm
    alpha = float((2 * depth) ** 0.25)

    x_all = jnp.concatenate([x.reshape(n_lat, d), ctx.reshape(batch * n_ctx, d)], axis=0)
    cond = jnp.zeros((SUBLANES, d), F32).at[:batch].set(c).at[batch].set(c_ctx)
    mods = _ada_call(cond, ada_w, ada_b)[:, :batch + 1].reshape(depth, batch + 1, 6, d)
    cos_t, sin_t = _rope_tables(seq, tm)
    x_spec, mod_spec = _row_specs(tm, d, tiles_per_sample, batch)
    vec = lambda a: a.reshape(1, -1)

    for i in range(depth):
        last = i == depth - 1
        kind, j = i % 3, i // 3
        mod = mods[i]
        n_rows = rows
        if kind == 0:
            w_qkv = attn_w_qkv[j].astype(BF16)
            kvw = N_KV_HEADS * HEAD_DIM
            rope_spec = pl.BlockSpec(
                (tm, LANES), lambda t: (jnp.where(t < batch * tiles_per_sample, t % tiles_per_sample,
                                                  tiles_per_sample), 0))
            q, k, v = pl.pallas_call(
                _attn_in_kernel,
                out_shape=(jax.ShapeDtypeStruct((rows, d), BF16),
                           jax.ShapeDtypeStruct((rows, kvw), BF16),
                           jax.ShapeDtypeStruct((rows, kvw), BF16)),
                grid=(rows // tm,),
                in_specs=[x_spec, mod_spec, _full(w_qkv.shape), rope_spec, rope_spec],
                out_specs=(x_spec, pl.BlockSpec((tm, kvw), lambda t: (t, 0)),
                           pl.BlockSpec((tm, kvw), lambda t: (t, 0))),
                compiler_params=_cparams("arbitrary"),
                name="attn_in_proj",
            )(x_all, mod, w_qkv, cos_t, sin_t)
            o = _attn_call(q, k, v, vec(attn_sink[j]), batch, seq, n_ctx)
            cores, w_out, b_out, pkind = [o], attn_w_o[j].astype(BF16), None, "direct"
        elif kind == 1:
            w_in = lru_w_in[j].astype(BF16)
            gate, rec = pl.pallas_call(
                _lru_in_kernel,
                out_shape=(jax.ShapeDtypeStruct((rows, d), F32), jax.ShapeDtypeStruct((rows, d), F32)),
                grid=(rows // tm,),
                in_specs=[x_spec, mod_spec, _full(w_in.shape)],
                out_specs=(x_spec, x_spec),
                compiler_params=_cparams("arbitrary"),
                name="lru_in_proj",
            )(x_all, mod, w_in)
            hf, hb = _lru_call(rec, lru_conv_w[j], vec(lru_conv_b[j]),
                               lru_gate_a_w[j].astype(BF16), lru_gate_a_b[j],
                               lru_gate_x_w[j].astype(BF16), lru_gate_x_b[j], lru_lambda[j],
                               batch, seq, n_ctx)
            cores, w_out, b_out, pkind = [hf, hb, gate], lru_w_out[j].astype(BF16), None, "lru"
        else:
            w_in = conv_w_in[j].astype(BF16)
            glu = pl.pallas_call(
                _conv_in_kernel,
                out_shape=jax.ShapeDtypeStruct((rows, d), F32),
                grid=(rows // tm,),
                in_specs=[x_spec, mod_spec, _full(w_in.shape), _full((1, 2 * d))],
                out_specs=x_spec,
                compiler_params=_cparams("arbitrary"),
                name="conv_in_proj",
            )(x_all, mod, w_in, vec(conv_b_in[j]))
            core = _dwconv_call(glu, conv_dw_w[j], vec(conv_dw_b[j]), vec(conv_ln_g[j]), vec(conv_ln_b[j]),
                                batch, seq, n_ctx)
            cores, w_out, b_out, pkind = [core], conv_w_out[j].astype(BF16), vec(conv_b_out[j]), "direct"
        post_rows = n_lat if last else rows
        x1, t, aff_t = _post_call(pkind, alpha, cores, w_out, b_out, x_all, mod, vec(ln1_g[i]), vec(ln1_b[i]),
                                  moe_router[i].T.astype(BF16), post_rows, tiles_per_sample, batch)
        x_all = _moe(alpha, x1, t, aff_t, mod, vec(ln2_g[i]), vec(ln2_b[i]),
                     moe_w1[i].astype(BF16), moe_w3[i].astype(BF16), moe_w2[i].astype(BF16),
                     batch, seq, n_ctx, not last, last)
    return x_all.reshape(batch, seq, d)
```

```python
import functools

import jax
import jax.numpy as jnp
from jax import lax
from jax.experimental import pallas as pl
from jax.experimental.pallas import tpu as pltpu

F32 = jnp.float32
BF16 = jnp.bfloat16
I32 = jnp.int32

HEAD_DIM = 64
N_KV_HEADS = 4
GROUP = 4
WINDOW = 128
GRID_W = 64
ROPE_BASE = 10000.0
N_EXPERTS = 16
EC_CAPACITY_FACTOR = 2
LRU_C = 8.0
LN_EPS = 1e-5
NEG_INF = -1e30

ROW_TILE = 512
Q_TILE = 128
SEQ_CHUNK = 256
TOKEN_BLOCK = 256
SLOT_TILE = 128
FF_TILE = 512
FFN_ROW_TILE = 512
COMBINE_GROUP = 32
MXU_DEPTH = 256
LANES = 128
SUBLANES = 8
BF16_ROWS = 16
VMEM_LIMIT = 56 * 1024 * 1024


def _cparams(*sem):
    return pltpu.CompilerParams(dimension_semantics=sem, vmem_limit_bytes=VMEM_LIMIT)


def _layer_norm(y, g, b):
    mu = jnp.mean(y, axis=-1, keepdims=True)
    d = y - mu
    var = jnp.mean(d * d, axis=-1, keepdims=True)
    return d * lax.rsqrt(var + LN_EPS) * g + b


def _dot(a, b):
    return jnp.dot(a, b, preferred_element_type=F32)


def _dot_nt(a, b):
    return lax.dot_general(a, b, (((1,), (1,)), ((), ())), preferred_element_type=F32)


def _dot_tn(a, b):
    return lax.dot_general(a, b, (((0,), (0,)), ((), ())), preferred_element_type=F32)


def _ada_kernel(c_ref, w_ref, b_ref, o_ref):
    cnd = c_ref[...]
    s = cnd * jax.nn.sigmoid(cnd)
    o_ref[0] = _dot(s.astype(BF16), w_ref[0].astype(BF16)) + b_ref[0]


def _ada_call(cond, ada_w, ada_b):
    depth, d, n6 = ada_w.shape
    rows = cond.shape[0]
    tn = 1536
    return pl.pallas_call(
        _ada_kernel,
        out_shape=jax.ShapeDtypeStruct((depth, rows, n6), F32),
        grid=(depth, n6 // tn),
        in_specs=[
            pl.BlockSpec((rows, d), lambda l, n: (0, 0)),
            pl.BlockSpec((1, d, tn), lambda l, n: (l, 0, n)),
            pl.BlockSpec((1, 1, tn), lambda l, n: (l, 0, n)),
        ],
        out_specs=pl.BlockSpec((1, rows, tn), lambda l, n: (l, 0, n)),
        compiler_params=_cparams("arbitrary", "arbitrary"),
        name="ada_mod",
    )(cond, ada_w, ada_b.reshape(depth, 1, n6))


def _modulate1(x_ref, mod_ref):
    m = mod_ref[0]
    return (x_ref[...] * (1.0 + m[1:2]) + m[0:1]).astype(BF16)


def _attn_in_kernel(x_ref, mod_ref, w_ref, cos_ref, sin_ref, q_ref, k_ref, v_ref):
    d = x_ref.shape[1]
    z = _dot(_modulate1(x_ref, mod_ref), w_ref[...])
    cos = cos_ref[...]
    sin = sin_ref[...]
    lane = lax.broadcasted_iota(I32, cos.shape, 1)
    first_half = (lane % HEAD_DIM) < (HEAD_DIM // 2)

    def rope(t):
        swapped = jnp.where(first_half, pltpu.roll(t, LANES - HEAD_DIM // 2, 1),
                            pltpu.roll(t, HEAD_DIM // 2, 1))
        return t * cos + swapped * sin

    scale = HEAD_DIM ** -0.5
    for cidx in range(d // LANES):
        sl = slice(cidx * LANES, (cidx + 1) * LANES)
        q_ref[:, sl] = (rope(z[:, sl]) * scale).astype(BF16)
    kvw = k_ref.shape[1]
    for cidx in range(kvw // LANES):
        sl = slice(cidx * LANES, (cidx + 1) * LANES)
        k_ref[:, sl] = rope(z[:, d + cidx * LANES:d + (cidx + 1) * LANES]).astype(BF16)
    v_ref[...] = z[:, d + kvw:].astype(BF16)


def _lru_in_kernel(x_ref, mod_ref, w_ref, g_ref, rec_ref):
    d = x_ref.shape[1]
    z = _dot(_modulate1(x_ref, mod_ref), w_ref[...])
    g_ref[...] = jax.nn.gelu(z[:, :d])
    rec_ref[...] = z[:, d:]


def _conv_in_kernel(x_ref, mod_ref, w_ref, b_ref, o_ref):
    d = x_ref.shape[1]
    z = _dot(_modulate1(x_ref, mod_ref), w_ref[...]) + b_ref[...]
    o_ref[...] = z[:, :d] * jax.nn.sigmoid(z[:, d:])


def _row_specs(tm, d, tiles_per_sample, n_samples):
    x_spec = pl.BlockSpec((tm, d), lambda i: (i, 0))
    mod_spec = pl.BlockSpec((1, 6, d), lambda i: (jnp.minimum(i // tiles_per_sample, n_samples), 0, 0))
    return x_spec, mod_spec


def _full(shape):
    n = len(shape)
    return pl.BlockSpec(shape, lambda *_: (0,) * n)


def _attn_kernel(seq, n_lat_steps, q_ref, kp_ref, kc_ref, kn_ref, vp_ref, vc_ref, vn_ref,
                 kx_ref, vx_ref, sink_ref, o_ref):
    i = pl.program_id(1)
    tq = q_ref.shape[0]
    n_ctx = kx_ref.shape[0]
    span = tq + 2 * WINDOW
    qpos = i * tq + lax.broadcasted_iota(I32, (tq, span), 0)
    kpos = i * tq - WINDOW + lax.broadcasted_iota(I32, (tq, span), 1)
    seq_eff = jnp.where(i < n_lat_steps, seq, 0)
    valid = (jnp.abs(qpos - kpos) <= WINDOW) & (kpos >= 0) & (kpos < seq_eff)
    bias = jnp.concatenate(
        [jnp.zeros((tq, n_ctx), F32), jnp.where(valid, 0.0, NEG_INF).astype(F32)], axis=1)
    n_keys = n_ctx + span
    for h in range(N_KV_HEADS):
        hs = slice(h * HEAD_DIM, (h + 1) * HEAD_DIM)
        kk = jnp.concatenate([kx_ref[:, hs], kp_ref[:, hs], kc_ref[:, hs], kn_ref[:, hs]], axis=0)
        vv = jnp.concatenate([vx_ref[:, hs], vp_ref[:, hs], vc_ref[:, hs], vn_ref[:, hs]], axis=0)
        heads = range(h * GROUP, (h + 1) * GROUP)
        qg = jnp.concatenate([q_ref[:, hq * HEAD_DIM:(hq + 1) * HEAD_DIM] for hq in heads], axis=0)
        sink = jnp.concatenate(
            [jnp.broadcast_to(sink_ref[0:1, hq:hq + 1], (1, tq, 1)) for hq in heads], axis=0)
        s = _dot_nt(qg, kk).reshape(GROUP, tq, n_keys) + bias[None]
        m = jnp.maximum(jnp.max(s, axis=-1, keepdims=True), sink)
        e = jnp.exp(s - m)
        den = jnp.sum(e, axis=-1, keepdims=True) + jnp.exp(sink - m)
        o = _dot(e.reshape(GROUP * tq, n_keys).astype(BF16), vv).reshape(GROUP, tq, HEAD_DIM) / den
        for g, hq in enumerate(heads):
            o_ref[:, hq * HEAD_DIM:(hq + 1) * HEAD_DIM] = o[g].astype(BF16)


def _attn_call(q, k, v, sink, batch, seq, n_ctx):
    rows, d = q.shape
    kvw = k.shape[1]
    tq = Q_TILE
    assert seq % tq == 0 and n_ctx % tq == 0 and tq % WINDOW == 0 and (batch * seq) % n_ctx == 0
    nl, nc = seq // tq, n_ctx // tq
    wb = seq // WINDOW
    r = tq // WINDOW

    def q_map(b, i):
        return (jnp.where(i < nl, b * nl + i, batch * nl + b * nc + (i - nl)), 0)

    def prev_map(b, i):
        return (b * wb + jnp.clip(i * r - 1, 0, wb - 1), 0)

    def cur_map(b, i):
        return (b * nl + jnp.minimum(i, nl - 1), 0)

    def next_map(b, i):
        return (b * wb + jnp.minimum((i + 1) * r, wb - 1), 0)

    def ctx_map(b, i):
        return ((batch * seq) // n_ctx + b, 0)

    win = lambda m: pl.BlockSpec((WINDOW, kvw), m)
    cur = pl.BlockSpec((tq, kvw), cur_map)
    ctxs = pl.BlockSpec((n_ctx, kvw), ctx_map)
    return pl.pallas_call(
        functools.partial(_attn_kernel, seq, nl),
        out_shape=jax.ShapeDtypeStruct((rows, d), BF16),
        grid=(batch, nl + nc),
        in_specs=[pl.BlockSpec((tq, d), q_map),
                  win(prev_map), cur, win(next_map), win(prev_map), cur, win(next_map),
                  ctxs, ctxs, _full(sink.shape)],
        out_specs=pl.BlockSpec((tq, d), q_map),
        compiler_params=_cparams("arbitrary", "arbitrary"),
        name="window_attn",
    )(q, k, k, k, v, v, v, k, v, sink)


def _shift_rows(x, k):
    return pltpu.roll(x, k % x.shape[0], 0)


def _lru_kernel(n_ctx_chunks, n_lat_chunks,
                pf_ref, cf_ref, nf_ref, pb_ref, cb_ref, nb_ref,
                cw_ref, cbias_ref, wa_ref, ba_ref, wx_ref, bx_ref, lam_ref,
                hf_ref, hb_ref, carry_ref):
    j = pl.program_id(1)
    tc, d = cf_ref.shape
    halo = pf_ref.shape[0]
    nblk = wa_ref.shape[1]
    bw = d // nblk
    is_ctx = j < n_ctx_chunks
    jf = jnp.where(is_ctx, j, j - n_ctx_chunks)
    nseq = jnp.where(is_ctx, n_ctx_chunks, n_lat_chunks)
    jb = nseq - 1 - jf

    @pl.when(j == 0)
    def _():
        carry_ref[...] = jnp.zeros(carry_ref.shape, F32)

    sub = lax.broadcasted_iota(I32, (tc, d), 0) % SUBLANES
    row_blk = lax.broadcasted_iota(I32, (tc, bw), 0)

    def coeffs(direction, p_ref, c_ref, n_ref, jj):
        prev = p_ref[...] * (jj > 0).astype(F32)
        nxt = n_ref[...] * (jj < nseq - 1).astype(F32)
        ext = jnp.concatenate([prev, c_ref[...], nxt], axis=0)
        xr = jnp.zeros((tc, d), F32) + cbias_ref[...]
        for k in range(cw_ref.shape[0]):
            xr = xr + cw_ref[k:k + 1, :] * _shift_rows(ext, 2 - k)[halo:halo + tc]
        a_parts, b_parts = [], []
        for n in range(nblk):
            cs = slice(n * bw, (n + 1) * bw)
            xb = xr[:, cs].astype(BF16)
            r = jax.nn.sigmoid(_dot(xb, wa_ref[direction, n]) + ba_ref[direction:direction + 1, cs])
            gi = jax.nn.sigmoid(_dot(xb, wx_ref[direction, n]) + bx_ref[direction:direction + 1, cs])
            log_a = -LRU_C * r * jax.nn.softplus(-lam_ref[direction:direction + 1, cs])
            a = jnp.exp(log_a)
            mult = jnp.sqrt(1.0 - a * a)
            at_edge = is_ctx & (jj == (0 if direction == 0 else nseq - 1))
            reset_row = jnp.where(at_edge, 0 if direction == 0 else tc - 1, -1)
            mult = jnp.where(row_blk == reset_row, 1.0, mult)
            a_parts.append(a)
            b_parts.append(mult * gi * xr[:, cs])
        return jnp.concatenate(a_parts, axis=1), jnp.concatenate(b_parts, axis=1)

    def scan(direction, a, bb, out_ref):
        for k in (1, 2, 4):
            if direction == 0:
                keep = sub >= k
                a_sh, b_sh = _shift_rows(a, k), _shift_rows(bb, k)
            else:
                keep = sub < SUBLANES - k
                a_sh, b_sh = _shift_rows(a, -k), _shift_rows(bb, -k)
            bb = jnp.where(keep, a * b_sh + bb, bb)
            a = jnp.where(keep, a * a_sh, a)
        carry = carry_ref[direction:direction + 1, :]
        ngroups = tc // SUBLANES
        order = range(ngroups) if direction == 0 else range(ngroups - 1, -1, -1)
        for gidx in order:
            rs = slice(gidx * SUBLANES, (gidx + 1) * SUBLANES)
            h = a[rs] * carry + bb[rs]
            out_ref[rs, :] = h
            carry = h[SUBLANES - 1:SUBLANES] if direction == 0 else h[0:1]
        carry_ref[direction:direction + 1, :] = carry

    a_f, b_f = coeffs(0, pf_ref, cf_ref, nf_ref, jf)
    scan(0, a_f, b_f, hf_ref)
    a_b, b_b = coeffs(1, pb_ref, cb_ref, nb_ref, jb)
    scan(1, a_b, b_b, hb_ref)


def _lru_call(rec, conv_w, conv_b, wa, ba, wx, bx, lam, batch, seq, n_ctx):
    rows, d = rec.shape
    tc = SEQ_CHUNK
    halo = SUBLANES
    assert seq % tc == 0 and n_ctx % tc == 0
    ncc, nlc = n_ctx // tc, seq // tc
    hb_per_chunk = tc // halo
    last_halo = rows // halo - 1

    def chunk_f(b, j):
        return jnp.where(j < ncc, (batch * seq) // tc + b * ncc + j, b * nlc + (j - ncc))

    def chunk_b(b, j):
        return jnp.where(j < ncc, (batch * seq) // tc + b * ncc + (ncc - 1 - j),
                         b * nlc + (nlc - 1 - (j - ncc)))

    def cur(f):
        return pl.BlockSpec((tc, d), lambda b, j: (f(b, j), 0))

    def prev(f):
        return pl.BlockSpec((halo, d), lambda b, j: (jnp.maximum(f(b, j) * hb_per_chunk - 1, 0), 0))

    def nxt(f):
        return pl.BlockSpec((halo, d),
                            lambda b, j: (jnp.minimum((f(b, j) + 1) * hb_per_chunk, last_halo), 0))

    out = jax.ShapeDtypeStruct((rows, d), F32)
    return pl.pallas_call(
        functools.partial(_lru_kernel, ncc, nlc),
        out_shape=(out, out),
        grid=(batch, ncc + nlc),
        in_specs=[prev(chunk_f), cur(chunk_f), nxt(chunk_f), prev(chunk_b), cur(chunk_b), nxt(chunk_b),
                  _full(conv_w.shape), _full(conv_b.shape), _full(wa.shape), _full(ba.shape),
                  _full(wx.shape), _full(bx.shape), _full(lam.shape)],
        out_specs=(cur(chunk_f), cur(chunk_b)),
        scratch_shapes=[pltpu.VMEM((2, d), F32)],
        compiler_params=_cparams("arbitrary", "arbitrary"),
        name="rglru_scan",
    )(rec, rec, rec, rec, rec, rec, conv_w, conv_b, wa, ba, wx, bx, lam)


def _dwconv_kernel(n_lat_chunks_total, lat_chunks, ctx_chunks,
                   p_ref, c_ref, n_ref, w_ref, b_ref, g_ref, beta_ref, o_ref):
    ci = pl.program_id(0)
    tc, d = c_ref.shape
    halo = p_ref.shape[0]
    ktaps = w_ref.shape[0]
    is_lat = ci < n_lat_chunks_total
    pos = jnp.where(is_lat, ci % lat_chunks, (ci - n_lat_chunks_total) % ctx_chunks)
    nseq = jnp.where(is_lat, lat_chunks, ctx_chunks)
    prev = p_ref[...] * (pos > 0).astype(F32)
    nxt = n_ref[...] * (pos < nseq - 1).astype(F32)
    ext = jnp.concatenate([prev, c_ref[...], nxt], axis=0)
    acc = jnp.zeros((tc, d), F32) + b_ref[...]
    rotated = [ext if r == 0 else _shift_rows(ext, r) for r in range(SUBLANES)]
    for k in range(ktaps):
        a, r = divmod(ktaps // 2 - k, SUBLANES)
        acc = acc + w_ref[k:k + 1, :] * rotated[r][halo - SUBLANES * a:halo - SUBLANES * a + tc]
    z = _layer_norm(acc, g_ref[...], beta_ref[...])
    o_ref[...] = (z * jax.nn.sigmoid(z)).astype(BF16)


def _dwconv_call(glu, w, b, g, beta, batch, seq, n_ctx):
    rows, d = glu.shape
    tc = SEQ_CHUNK
    halo = 2 * SUBLANES
    assert w.shape[0] // 2 <= halo and seq % tc == 0 and n_ctx % tc == 0
    per = tc // halo
    last_halo = rows // halo - 1
    return pl.pallas_call(
        functools.partial(_dwconv_kernel, batch * seq // tc, seq // tc, n_ctx // tc),
        out_shape=jax.ShapeDtypeStruct((rows, d), BF16),
        grid=(rows // tc,),
        in_specs=[pl.BlockSpec((halo, d), lambda i: (jnp.maximum(i * per - 1, 0), 0)),
                  pl.BlockSpec((tc, d), lambda i: (i, 0)),
                  pl.BlockSpec((halo, d), lambda i: (jnp.minimum((i + 1) * per, last_halo), 0)),
                  _full(w.shape), _full(b.shape), _full(g.shape), _full(beta.shape)],
        out_specs=pl.BlockSpec((tc, d), lambda i: (i, 0)),
        compiler_params=_cparams("arbitrary"),
        name="dwconv_ln_swish",
    )(glu, glu, glu, w, b, g, beta)


def _post_kernel(kind, has_bias, alpha, *refs):
    refs = list(refs)
    if kind == "lru":
        hf_ref, hb_ref, gate_ref = refs[:3]
        refs = refs[3:]
        core = ((hf_ref[...] + hb_ref[...]) * gate_ref[...]).astype(BF16)
    else:
        core = refs[0][...]
        refs = refs[1:]
    w_ref = refs[0]
    refs = refs[1:]
    y = _dot(core, w_ref[...])
    if has_bias:
        y = y + refs[0][...]
        refs = refs[1:]
    x_ref, mod_ref, lng_ref, lnb_ref, rt_ref, x1_ref, t_ref, aff_ref = refs
    m = mod_ref[0]
    x1 = _layer_norm(alpha * x_ref[...] + m[2:3] * y, lng_ref[...], lnb_ref[...])
    x1_ref[...] = x1
    t = (x1 * (1.0 + m[4:5]) + m[3:4]).astype(BF16)
    t_ref[...] = t
    logits = _dot_nt(rt_ref[...], t)
    e = jnp.exp(logits - jnp.max(logits, axis=0, keepdims=True))
    aff_ref[...] = e / jnp.sum(e, axis=0, keepdims=True)


def _post_call(kind, alpha, cores, w, bias, x_all, mod, ln_g, ln_b, router_t, n_rows, tiles_per_sample,
               n_samples):
    d = x_all.shape[1]
    tm = ROW_TILE
    n_e = router_t.shape[0]
    x_spec, mod_spec = _row_specs(tm, d, tiles_per_sample, n_samples)
    in_specs = [x_spec] * len(cores) + [_full(w.shape)]
    args = list(cores) + [w]
    if bias is not None:
        in_specs.append(_full(bias.shape))
        args.append(bias)
    in_specs += [x_spec, mod_spec, _full(ln_g.shape), _full(ln_b.shape), _full(router_t.shape)]
    args += [x_all, mod, ln_g, ln_b, router_t]
    return pl.pallas_call(
        functools.partial(_post_kernel, kind, bias is not None, alpha),
        out_shape=(jax.ShapeDtypeStruct((n_rows, d), F32),
                   jax.ShapeDtypeStruct((n_rows, d), BF16),
                   jax.ShapeDtypeStruct((n_e, n_rows), F32)),
        grid=(n_rows // tm,),
        in_specs=in_specs,
        out_specs=(x_spec, x_spec, pl.BlockSpec((n_e, tm), lambda i: (0, i))),
        compiler_params=_cparams("arbitrary"),
        name="mixer_out_ln_router",
    )(*args)


def _route_kernel(cap, slot_tile, aff_ref, pos_ref, excl_ref, lo_ref, hi_ref):
    n_e, seq = aff_ref.shape
    tb = min(TOKEN_BLOCK, seq)
    nblk = seq // tb
    bits = pltpu.bitcast(aff_ref[...], I32)

    def search(_, lohi):
        lo, hi = lohi
        mid = lo + jnp.right_shift(hi - lo, 1)
        cnt = jnp.sum((bits >= mid).astype(F32), axis=1, keepdims=True)
        ge = cnt >= cap
        return jnp.where(ge, mid, lo), jnp.where(ge, hi, mid)

    lo0 = jnp.zeros((n_e, 1), I32)
    hi0 = jnp.full((n_e, 1), 0x7F800000, I32)
    thr, _ = lax.fori_loop(0, 31, search, (lo0, hi0))
    n_gt = jnp.sum((bits > thr).astype(F32), axis=1, keepdims=True)
    need = cap - n_gt

    upper = (lax.broadcasted_iota(I32, (tb, tb), 0) < lax.broadcasted_iota(I32, (tb, tb), 1)).astype(BF16)
    lane = lax.broadcasted_iota(I32, (n_e, LANES), 1)
    eq_off = jnp.zeros((n_e, 1), F32)
    pos_off = jnp.zeros((n_e, 1), F32)
    excl = jnp.full((n_e, LANES), float(cap), F32)
    incl = jnp.full((n_e, LANES), 1e9, F32)
    for blk in range(nblk):
        cs = slice(blk * tb, (blk + 1) * tb)
        bb = bits[:, cs]
        gt = bb > thr
        eq = bb == thr
        eqf = jnp.where(eq, 1.0, 0.0)
        rank = _dot(eqf.astype(BF16), upper) + eq_off
        sel = gt | (eq & (rank < need))
        self_f = jnp.where(sel, 1.0, 0.0)
        pin = _dot(self_f.astype(BF16), upper) + pos_off
        pos_ref[:, cs] = jnp.where(sel, pin, -1.0)
        excl = jnp.where(lane == blk, pos_off, excl)
        eq_off = eq_off + jnp.sum(eqf, axis=1, keepdims=True)
        pos_off = pos_off + jnp.sum(self_f, axis=1, keepdims=True)
        incl = jnp.where(lane == blk, pos_off, incl)
    excl_ref[0] = excl.astype(I32)
    lo = jnp.zeros((n_e, LANES), F32)
    hi = jnp.zeros((n_e, LANES), F32)
    for q in range(cap // slot_tile):
        first = float(q * slot_tile)
        last = float(q * slot_tile + slot_tile - 1)
        lo_q = jnp.sum(jnp.where(incl <= first, 1.0, 0.0), axis=1, keepdims=True)
        hi_q = jnp.sum(jnp.where(incl <= last, 1.0, 0.0), axis=1, keepdims=True)
        lo = jnp.where(lane == q, lo_q, lo)
        hi = jnp.where(lane == q, jnp.minimum(hi_q, float(nblk - 1)), hi)
    lo_ref[0] = lo.astype(I32)
    hi_ref[0] = hi.astype(I32)


def _route_call(aff_t, batch, seq, first_block, cap, slot_tile):
    n_e = aff_t.shape[0]
    assert seq // min(TOKEN_BLOCK, seq) < LANES and cap // slot_tile <= LANES
    meta = jax.ShapeDtypeStruct((batch, n_e, LANES), I32)
    meta_spec = pl.BlockSpec((1, n_e, LANES), lambda b: (b, 0, 0))
    return pl.pallas_call(
        functools.partial(_route_kernel, cap, slot_tile),
        out_shape=(jax.ShapeDtypeStruct((n_e, batch * seq), F32), meta, meta, meta),
        grid=(batch,),
        in_specs=[pl.BlockSpec((n_e, seq), lambda b: (0, first_block + b))],
        out_specs=(pl.BlockSpec((n_e, seq), lambda b: (0, b)), meta_spec, meta_spec, meta_spec),
        compiler_params=_cparams("arbitrary"),
        name="expert_choice_route",
    )(aff_t)


def _dispatch_kernel(slot_tile, lo_ref, hi_ref, t_ref, pos_ref, aff_ref, xg_ref, gate_ref):
    b = pl.program_id(0)
    e = pl.program_id(1)
    n_e = pl.num_programs(1)
    cap, d = xg_ref.shape[2], xg_ref.shape[3]
    tb = pos_ref.shape[3]
    nq = cap // slot_tile
    slot = lax.broadcasted_iota(I32, (slot_tile, tb), 0).astype(F32)
    for q in range(nq):
        base = (b * n_e + e) * nq + q
        lo = lo_ref[base]
        hi = hi_ref[base]

        def body(blk, carry):
            acc, gacc = carry
            pos = pos_ref[0, blk]
            hit = (pos - float(q * slot_tile)) == slot
            x_blk = t_ref[pl.ds(pl.multiple_of(blk * tb, tb), tb), :]
            acc = acc + _dot(jnp.where(hit, 1.0, 0.0).astype(BF16), x_blk)
            gacc = gacc + jnp.sum(jnp.where(hit, aff_ref[0, blk], 0.0), axis=1, keepdims=True)
            return acc, gacc

        acc, gacc = lax.fori_loop(lo, hi + 1, body,
                                  (jnp.zeros((slot_tile, d), F32), jnp.zeros((slot_tile, 1), F32)))
        xg_ref[0, 0, q * slot_tile:(q + 1) * slot_tile, :] = acc.astype(BF16)
        gate_ref[0, 0, q * slot_tile:(q + 1) * slot_tile, :] = gacc


def _dispatch_call(t_all, pos, aff_t, lo, hi, batch, seq, first_seq_block, aff_first_block, cap, slot_tile):
    d = t_all.shape[1]
    n_e = pos.shape[0]
    tb = min(TOKEN_BLOCK, seq)
    nblk = seq // tb
    pos4 = pos.reshape(n_e, batch * nblk, 1, tb)
    aff4 = aff_t.reshape(n_e, aff_t.shape[1] // tb, 1, tb)
    grid_spec = pltpu.PrefetchScalarGridSpec(
        num_scalar_prefetch=2,
        grid=(batch, n_e),
        in_specs=[
            pl.BlockSpec((seq, d), lambda b, e, *_: (first_seq_block + b, 0), pipeline_mode=pl.Buffered(1)),
            pl.BlockSpec((1, nblk, 1, tb), lambda b, e, *_: (e, b, 0, 0)),
            pl.BlockSpec((1, nblk, 1, tb), lambda b, e, *_: (e, aff_first_block + b, 0, 0)),
        ],
        out_specs=(pl.BlockSpec((1, 1, cap, d), lambda b, e, *_: (b, e, 0, 0)),
                   pl.BlockSpec((1, 1, cap, 1), lambda b, e, *_: (b, e, 0, 0))),
    )
    return pl.pallas_call(
        functools.partial(_dispatch_kernel, slot_tile),
        out_shape=(jax.ShapeDtypeStruct((batch, n_e, cap, d), BF16),
                   jax.ShapeDtypeStruct((batch, n_e, cap, 1), F32)),
        grid_spec=grid_spec,
        compiler_params=_cparams("arbitrary", "arbitrary"),
        name="moe_dispatch",
    )(lo.reshape(-1), hi.reshape(-1), t_all, pos4, aff4)


def _ffn_kernel(n_streams, w1_ref, w3_ref, w2_ref, *refs):
    ins, outs, accs = refs[:2 * n_streams], refs[2 * n_streams:3 * n_streams], refs[3 * n_streams:]
    f = pl.program_id(2)
    w1 = w1_ref[0, 0].astype(BF16)
    w3 = w3_ref[0, 0].astype(BF16)
    w2 = w2_ref[0, 0].astype(BF16)
    for s in range(n_streams):
        xg_ref, gate_ref, y_ref, acc_ref = ins[2 * s], ins[2 * s + 1], outs[s], accs[s]
        cap = acc_ref.shape[0]
        rt = min(FFN_ROW_TILE, cap)

        @pl.when(f == 0)
        def _(acc_ref=acc_ref):
            acc_ref[...] = jnp.zeros(acc_ref.shape, F32)

        for r in range(cap // rt):
            rows = slice(r * rt, (r + 1) * rt)
            xg = xg_ref[0, 0, rows, :]
            h1 = _dot(xg, w1)
            h3 = _dot(xg, w3)
            hid = (h1 * jax.nn.sigmoid(h1) * h3).astype(BF16)
            acc_ref[rows, :] += _dot(hid, w2)

        @pl.when(f == pl.num_programs(2) - 1)
        def _(acc_ref=acc_ref, gate_ref=gate_ref, y_ref=y_ref):
            y_ref[0, 0] = (acc_ref[...] * gate_ref[0, 0]).astype(BF16)


def _ffn_call(layer, streams, w1, w3, w2):
    n_e, d, ff = w1.shape[1:]
    batch = streams[0][0].shape[0]
    tf = FF_TILE
    tok = lambda e, b, f: (b, e, 0, 0)
    in_specs = [pl.BlockSpec((1, 1, d, tf), lambda e, b, f: (layer, e, 0, f)),
                pl.BlockSpec((1, 1, d, tf), lambda e, b, f: (layer, e, 0, f)),
                pl.BlockSpec((1, 1, tf, d), lambda e, b, f: (layer, e, f, 0))]
    args, out_shape, out_specs, scratch = [w1, w3, w2], [], [], []
    for xg, gate in streams:
        cap = xg.shape[2]
        assert cap % min(FFN_ROW_TILE, cap) == 0
        in_specs += [pl.BlockSpec((1, 1, cap, d), tok), pl.BlockSpec((1, 1, cap, 1), tok)]
        args += [xg, gate]
        out_shape.append(jax.ShapeDtypeStruct(xg.shape, BF16))
        out_specs.append(pl.BlockSpec((1, 1, cap, d), tok))
        scratch.append(pltpu.VMEM((cap, d), F32))
    return pl.pallas_call(
        functools.partial(_ffn_kernel, len(streams)),
        out_shape=tuple(out_shape),
        grid=(n_e, batch, ff // tf),
        in_specs=in_specs,
        out_specs=tuple(out_specs),
        scratch_shapes=scratch,
        compiler_params=_cparams("arbitrary", "arbitrary", "arbitrary"),
        name="expert_ffn",
    )(*args)


def _combine_kernel(n_e, group, nblk, has_alias, alpha, excl_ref, *refs):
    if has_alias:
        refs = refs[1:]
    x1_ref, pos_ref, mod_ref, lng_ref, lnb_ref, y_hbm, o_ref, buf_ref, sem_ref, base_ref, exp_ref = refs
    b = pl.program_id(0)
    i = pl.program_id(1)
    tb, d = x1_ref.shape
    per_tile = MXU_DEPTH // group

    def group_copy(e, base, g):
        return pltpu.make_async_copy(
            y_hbm.at[b, e, pl.ds(pl.multiple_of(base, group), group)],
            buf_ref.at[pl.ds(pl.multiple_of(g * group, group), group)], sem_ref.at[0])

    @pl.when((b == 0) & (i == 0))
    def _():
        buf_ref[...] = jnp.zeros(buf_ref.shape, BF16)

    @pl.when(i < nblk)
    def _():
        n_groups = jnp.int32(0)
        for e in range(n_e):
            first = excl_ref[(b * n_e + e) * LANES + i]
            end = excl_ref[(b * n_e + e) * LANES + i + 1]
            start = (first // group) * group
            n_run = jnp.where(end > first, (end - start + group - 1) // group, 0)

            def issue(k, g, e=e, start=start):
                base = start + k * group
                base_ref[g] = base
                exp_ref[g] = e
                group_copy(e, base, g).start()
                return g + 1

            n_groups = lax.fori_loop(0, n_run, issue, n_groups)

        def wait_one(_, carry):
            group_copy(0, 0, 0).wait()
            return carry

        lax.fori_loop(0, n_groups, wait_one, 0)
        row = lax.broadcasted_iota(I32, (group, tb), 0).astype(F32)

        def tile(kt, f):
            parts = []
            for j in range(per_tile):
                g = kt * per_tile + j
                live = g < n_groups
                gc = jnp.minimum(g, n_groups - 1)
                base = jnp.where(live, base_ref[gc], -2 * group).astype(F32)
                hit = (pos_ref[exp_ref[gc]] - base) == row
                parts.append(jnp.where(hit, 1.0, 0.0).astype(BF16))
            onehot = jnp.concatenate(parts, axis=0)
            rows = buf_ref[pl.ds(pl.multiple_of(kt * MXU_DEPTH, MXU_DEPTH), MXU_DEPTH), :]
            return f + _dot_tn(onehot, rows)

        n_tiles = (n_groups + per_tile - 1) // per_tile
        f = lax.fori_loop(0, n_tiles, tile, jnp.zeros((tb, d), F32))
        m = mod_ref[0]
        o_ref[...] = _layer_norm(alpha * x1_ref[...] + m[5:6] * f, lng_ref[...], lnb_ref[...])

    @pl.when(i >= nblk)
    def _():
        o_ref[...] = x1_ref[...]


def _combine_call(alpha, x1, pos, excl, y, mod, ln_g, ln_b, batch, seq, x_first_block, mod_row, out_rows,
                  alias_buf, carry_first_block=0, carry_blocks=0):
    d = x1.shape[1]
    n_e, cap = y.shape[1], y.shape[2]
    tb = min(TOKEN_BLOCK, seq)
    nblk = seq // tb
    group = COMBINE_GROUP
    assert cap % group == 0 and MXU_DEPTH % group == 0 and group % BF16_ROWS == 0
    max_groups = n_e * min(tb // group + 1, cap // group)
    buf_rows = -(-max_groups * group // MXU_DEPTH) * MXU_DEPTH
    has_alias = alias_buf is not None

    def row_map(b, i, *_):
        return (jnp.where(i < nblk, x_first_block + b * nblk + i,
                          carry_first_block + b * carry_blocks + (i - nblk)), 0)

    mod_map = (lambda b, i, *_: (b, 0, 0)) if mod_row is None else (lambda b, i, *_: (mod_row, 0, 0))
    in_specs = [pl.BlockSpec((tb, d), row_map),
                pl.BlockSpec((n_e, 1, tb), lambda b, i, *_: (0, 0, b * nblk + jnp.minimum(i, nblk - 1))),
                pl.BlockSpec((1, 6, d), mod_map),
                _full(ln_g.shape), _full(ln_b.shape),
                pl.BlockSpec(memory_space=pl.ANY)]
    args = [x1, pos.reshape(n_e, 1, -1), mod, ln_g, ln_b, y]
    aliases = {}
    if has_alias:
        in_specs = [pl.BlockSpec(memory_space=pl.ANY)] + in_specs
        args = [alias_buf] + args
        aliases = {1: 0}
    grid_spec = pltpu.PrefetchScalarGridSpec(
        num_scalar_prefetch=1,
        grid=(batch, nblk + carry_blocks),
        in_specs=in_specs,
        out_specs=pl.BlockSpec((tb, d), row_map),
        scratch_shapes=[pltpu.VMEM((buf_rows, d), BF16), pltpu.SemaphoreType.DMA((1,)),
                        pltpu.SMEM((max_groups,), I32), pltpu.SMEM((max_groups,), I32)],
    )
    return pl.pallas_call(
        functools.partial(_combine_kernel, n_e, group, nblk, has_alias, alpha),
        out_shape=jax.ShapeDtypeStruct((out_rows, d), F32),
        grid_spec=grid_spec,
        input_output_aliases=aliases,
        compiler_params=_cparams("arbitrary", "arbitrary"),
        name="moe_combine_ln",
    )(excl.reshape(-1), *args)


def _rope_tables(seq, pad_rows):
    rows = seq // GRID_W
    row = jnp.broadcast_to(jnp.arange(rows, dtype=F32)[:, None], (rows, GRID_W)).reshape(-1)
    col = jnp.broadcast_to(jnp.arange(GRID_W, dtype=F32)[None, :], (rows, GRID_W)).reshape(-1)
    n_freq = HEAD_DIM // 4
    inv = ROPE_BASE ** (-jnp.arange(n_freq, dtype=F32) / n_freq)
    ang = jnp.concatenate([row[:, None] * inv, col[:, None] * inv], axis=-1)
    cos, sin = jnp.cos(ang), jnp.sin(ang)
    reps = LANES // HEAD_DIM
    cos_t = jnp.tile(jnp.concatenate([cos, cos], axis=-1), (1, reps))
    sin_t = jnp.tile(jnp.concatenate([-sin, sin], axis=-1), (1, reps))
    cos_t = jnp.concatenate([cos_t, jnp.ones((pad_rows, LANES), F32)], axis=0)
    sin_t = jnp.concatenate([sin_t, jnp.zeros((pad_rows, LANES), F32)], axis=0)
    return cos_t, sin_t


def _moe(alpha, layer, x1, t, aff_t, mod, ln_g, ln_b, w1, w3, w2, batch, seq, n_ctx, with_ctx, last):
    n_e = aff_t.shape[0]
    n_lat = batch * seq
    cap = max(1, EC_CAPACITY_FACTOR * seq // n_e)
    st = min(SLOT_TILE, cap)
    pos, excl, lo, hi = _route_call(aff_t, batch, seq, 0, cap, st)
    streams = [_dispatch_call(t, pos, aff_t, lo[:, :, :cap // st], hi[:, :, :cap // st],
                              batch, seq, 0, 0, cap, st)]
    if with_ctx:
        capc = max(1, EC_CAPACITY_FACTOR * n_ctx // n_e)
        stc = min(SLOT_TILE, capc)
        tbc = min(TOKEN_BLOCK, n_ctx)
        posc, exclc, loc, hic = _route_call(aff_t, batch, n_ctx, n_lat // n_ctx, capc, stc)
        streams.append(_dispatch_call(t, posc, aff_t, loc[:, :, :capc // stc], hic[:, :, :capc // stc],
                                      batch, n_ctx, n_lat // n_ctx, n_lat // tbc, capc, stc))
    ys = _ffn_call(layer, streams, w1, w3, w2)
    out_rows = n_lat if last else x1.shape[0]
    tb = min(TOKEN_BLOCK, seq)
    assert last or n_ctx % tb == 0
    x2 = _combine_call(alpha, x1, pos, excl, ys[0], mod, ln_g, ln_b, batch, seq, 0, None, out_rows, None,
                       carry_first_block=n_lat // tb, carry_blocks=0 if last else n_ctx // tb)
    if with_ctx:
        x2 = _combine_call(alpha, x1, posc, exclc, ys[1], mod, ln_g, ln_b, batch, n_ctx, n_lat // tbc, batch,
                           x1.shape[0], x2)
    return x2


def kernel(x, c, ctx, c_ctx, ada_w, ada_b, ln1_g, ln1_b, ln2_g, ln2_b, attn_w_qkv, attn_sink, attn_w_o, lru_w_in, lru_conv_w, lru_conv_b, lru_gate_a_w, lru_gate_a_b, lru_gate_x_w, lru_gate_x_b, lru_lambda, lru_w_out, conv_w_in, conv_b_in, conv_dw_w, conv_dw_b, conv_ln_g, conv_ln_b, conv_w_out, conv_b_out, moe_router, moe_w1, moe_w3, moe_w2):
    batch, seq, d = x.shape
    n_ctx = ctx.shape[1]
    depth = ada_w.shape[0]
    n_lat = batch * seq
    rows = n_lat + batch * n_ctx
    tm = ROW_TILE
    assert seq % tm == 0 and (batch * n_ctx) % tm == 0
    tiles_per_sample = seq // tm
    alpha = float((2 * depth) ** 0.25)

    x_all = jnp.concatenate([x.reshape(n_lat, d), ctx.reshape(batch * n_ctx, d)], axis=0)
    cond = jnp.zeros((SUBLANES, d), F32).at[:batch].set(c).at[batch].set(c_ctx)
    mods = _ada_call(cond, ada_w, ada_b)[:, :batch + 1].reshape(depth, batch + 1, 6, d)
    cos_t, sin_t = _rope_tables(seq, tm)
    x_spec, mod_spec = _row_specs(tm, d, tiles_per_sample, batch)
    vec = lambda a: a.reshape(1, -1)

    for i in range(depth):
        last = i == depth - 1
        kind, j = i % 3, i // 3
        mod = mods[i]
        n_rows = rows
        if kind == 0:
            w_qkv = attn_w_qkv[j].astype(BF16)
            kvw = N_KV_HEADS * HEAD_DIM
            rope_spec = pl.BlockSpec(
                (tm, LANES), lambda t: (jnp.where(t < batch * tiles_per_sample, t % tiles_per_sample,
                                                  tiles_per_sample), 0))
            q, k, v = pl.pallas_call(
                _attn_in_kernel,
                out_shape=(jax.ShapeDtypeStruct((rows, d), BF16),
                           jax.ShapeDtypeStruct((rows, kvw), BF16),
                           jax.ShapeDtypeStruct((rows, kvw), BF16)),
                grid=(rows // tm,),
                in_specs=[x_spec, mod_spec, _full(w_qkv.shape), rope_spec, rope_spec],
                out_specs=(x_spec, pl.BlockSpec((tm, kvw), lambda t: (t, 0)),
                           pl.BlockSpec((tm, kvw), lambda t: (t, 0))),
                compiler_params=_cparams("arbitrary"),
                name="attn_in_proj",
            )(x_all, mod, w_qkv, cos_t, sin_t)
            o = _attn_call(q, k, v, vec(attn_sink[j]), batch, seq, n_ctx)
            cores, w_out, b_out, pkind = [o], attn_w_o[j].astype(BF16), None, "direct"
        elif kind == 1:
            w_in = lru_w_in[j].astype(BF16)
            gate, rec = pl.pallas_call(
                _lru_in_kernel,
                out_shape=(jax.ShapeDtypeStruct((rows, d), F32), jax.ShapeDtypeStruct((rows, d), F32)),
                grid=(rows // tm,),
                in_specs=[x_spec, mod_spec, _full(w_in.shape)],
                out_specs=(x_spec, x_spec),
                compiler_params=_cparams("arbitrary"),
                name="lru_in_proj",
            )(x_all, mod, w_in)
            hf, hb = _lru_call(rec, lru_conv_w[j], vec(lru_conv_b[j]),
                               lru_gate_a_w[j].astype(BF16), lru_gate_a_b[j],
                               lru_gate_x_w[j].astype(BF16), lru_gate_x_b[j], lru_lambda[j],
                               batch, seq, n_ctx)
            cores, w_out, b_out, pkind = [hf, hb, gate], lru_w_out[j].astype(BF16), None, "lru"
        else:
            w_in = conv_w_in[j].astype(BF16)
            glu = pl.pallas_call(
                _conv_in_kernel,
                out_shape=jax.ShapeDtypeStruct((rows, d), F32),
                grid=(rows // tm,),
                in_specs=[x_spec, mod_spec, _full(w_in.shape), _full((1, 2 * d))],
                out_specs=x_spec,
                compiler_params=_cparams("arbitrary"),
                name="conv_in_proj",
            )(x_all, mod, w_in, vec(conv_b_in[j]))
            core = _dwconv_call(glu, conv_dw_w[j], vec(conv_dw_b[j]), vec(conv_ln_g[j]), vec(conv_ln_b[j]),
                                batch, seq, n_ctx)
            cores, w_out, b_out, pkind = [core], conv_w_out[j].astype(BF16), vec(conv_b_out[j]), "direct"
        post_rows = n_lat if last else rows
        x1, t, aff_t = _post_call(pkind, alpha, cores, w_out, b_out, x_all, mod, vec(ln1_g[i]), vec(ln1_b[i]),
                                  moe_router[i].T.astype(BF16), post_rows, tiles_per_sample, batch)
        x_all = _moe(alpha, i, x1, t, aff_t, mod, vec(ln2_g[i]), vec(ln2_b[i]), moe_w1, moe_w3, moe_w2,
                     batch, seq, n_ctx, not last, last)
    return x_all.reshape(batch, seq, d)
```

```python
import functools

import jax
import jax.numpy as jnp
from jax import lax
from jax.experimental import pallas as pl
from jax.experimental.pallas import tpu as pltpu

F32 = jnp.float32
BF16 = jnp.bfloat16
I32 = jnp.int32

HEAD_DIM = 64
N_KV_HEADS = 4
GROUP = 4
WINDOW = 128
GRID_W = 64
ROPE_BASE = 10000.0
N_EXPERTS = 16
EC_CAPACITY_FACTOR = 2
LRU_C = 8.0
LN_EPS = 1e-5
NEG_INF = -1e30

ROW_TILE = 512
Q_TILE = 128
SEQ_CHUNK = 256
SCAN_RUN_PAD = 8
TOKEN_BLOCK = 256
SLOT_TILE = 64
FF_TILE = 512
FFN_ROW_TILE = 512
COMBINE_GROUP = 32
MXU_DEPTH = 256
LANES = 128
SUBLANES = 8
BF16_ROWS = 16
VMEM_LIMIT = 56 * 1024 * 1024


def _cparams(*sem):
    return pltpu.CompilerParams(dimension_semantics=sem, vmem_limit_bytes=VMEM_LIMIT)


def _layer_norm(y, g, b):
    mu = jnp.mean(y, axis=-1, keepdims=True)
    d = y - mu
    var = jnp.mean(d * d, axis=-1, keepdims=True)
    return d * lax.rsqrt(var + LN_EPS) * g + b


def _dot(a, b):
    return jnp.dot(a, b, preferred_element_type=F32)


def _dot_nt(a, b):
    return lax.dot_general(a, b, (((1,), (1,)), ((), ())), preferred_element_type=F32)


def _dot_tn(a, b):
    return lax.dot_general(a, b, (((0,), (0,)), ((), ())), preferred_element_type=F32)


def _ada_kernel(c_ref, w_ref, b_ref, o_ref):
    cnd = c_ref[...]
    s = cnd * jax.nn.sigmoid(cnd)
    o_ref[0] = _dot(s.astype(BF16), w_ref[0].astype(BF16)) + b_ref[0]


def _ada_call(cond, ada_w, ada_b):
    depth, d, n6 = ada_w.shape
    rows = cond.shape[0]
    tn = 1536
    return pl.pallas_call(
        _ada_kernel,
        out_shape=jax.ShapeDtypeStruct((depth, rows, n6), F32),
        grid=(depth, n6 // tn),
        in_specs=[
            pl.BlockSpec((rows, d), lambda l, n: (0, 0)),
            pl.BlockSpec((1, d, tn), lambda l, n: (l, 0, n)),
            pl.BlockSpec((1, 1, tn), lambda l, n: (l, 0, n)),
        ],
        out_specs=pl.BlockSpec((1, rows, tn), lambda l, n: (l, 0, n)),
        compiler_params=_cparams("arbitrary", "arbitrary"),
        name="ada_mod",
    )(cond, ada_w, ada_b.reshape(depth, 1, n6))


def _modulate1(x_ref, mod_ref):
    m = mod_ref[0]
    return (x_ref[...] * (1.0 + m[1:2]) + m[0:1]).astype(BF16)


def _attn_in_kernel(x_ref, mod_ref, w_ref, cos_ref, sin_ref, q_ref, k_ref, v_ref):
    d = x_ref.shape[1]
    z = _dot(_modulate1(x_ref, mod_ref), w_ref[...])
    cos = cos_ref[...]
    sin = sin_ref[...]
    lane = lax.broadcasted_iota(I32, cos.shape, 1)
    first_half = (lane % HEAD_DIM) < (HEAD_DIM // 2)

    def rope(t):
        swapped = jnp.where(first_half, pltpu.roll(t, LANES - HEAD_DIM // 2, 1),
                            pltpu.roll(t, HEAD_DIM // 2, 1))
        return t * cos + swapped * sin

    scale = HEAD_DIM ** -0.5
    for cidx in range(d // LANES):
        sl = slice(cidx * LANES, (cidx + 1) * LANES)
        q_ref[:, sl] = (rope(z[:, sl]) * scale).astype(BF16)
    kvw = k_ref.shape[1]
    for cidx in range(kvw // LANES):
        sl = slice(cidx * LANES, (cidx + 1) * LANES)
        k_ref[:, sl] = rope(z[:, d + cidx * LANES:d + (cidx + 1) * LANES]).astype(BF16)
    v_ref[...] = z[:, d + kvw:].astype(BF16)


def _lru_in_kernel(x_ref, mod_ref, w_ref, g_ref, rec_ref):
    d = x_ref.shape[1]
    z = _dot(_modulate1(x_ref, mod_ref), w_ref[...])
    g_ref[...] = jax.nn.gelu(z[:, :d])
    rec_ref[...] = z[:, d:]


def _conv_in_kernel(x_ref, mod_ref, w_ref, b_ref, o_ref):
    d = x_ref.shape[1]
    z = _dot(_modulate1(x_ref, mod_ref), w_ref[...]) + b_ref[...]
    o_ref[...] = z[:, :d] * jax.nn.sigmoid(z[:, d:])


def _row_specs(tm, d, tiles_per_sample, n_samples):
    x_spec = pl.BlockSpec((tm, d), lambda i: (i, 0))
    mod_spec = pl.BlockSpec((1, 6, d), lambda i: (jnp.minimum(i // tiles_per_sample, n_samples), 0, 0))
    return x_spec, mod_spec


def _full(shape):
    n = len(shape)
    return pl.BlockSpec(shape, lambda *_: (0,) * n)


def _attn_kernel(seq, n_lat_steps, q_ref, kp_ref, kc_ref, kn_ref, vp_ref, vc_ref, vn_ref,
                 kx_ref, vx_ref, sink_ref, o_ref):
    i = pl.program_id(1)
    tq = q_ref.shape[0]
    n_ctx = kx_ref.shape[0]
    span = tq + 2 * WINDOW
    qpos = i * tq + lax.broadcasted_iota(I32, (tq, span), 0)
    kpos = i * tq - WINDOW + lax.broadcasted_iota(I32, (tq, span), 1)
    seq_eff = jnp.where(i < n_lat_steps, seq, 0)
    valid = (jnp.abs(qpos - kpos) <= WINDOW) & (kpos >= 0) & (kpos < seq_eff)
    bias = jnp.concatenate(
        [jnp.zeros((tq, n_ctx), F32), jnp.where(valid, 0.0, NEG_INF).astype(F32)], axis=1)
    n_keys = n_ctx + span
    for h in range(N_KV_HEADS):
        hs = slice(h * HEAD_DIM, (h + 1) * HEAD_DIM)
        kk = jnp.concatenate([kx_ref[:, hs], kp_ref[:, hs], kc_ref[:, hs], kn_ref[:, hs]], axis=0)
        vv = jnp.concatenate([vx_ref[:, hs], vp_ref[:, hs], vc_ref[:, hs], vn_ref[:, hs]], axis=0)
        heads = range(h * GROUP, (h + 1) * GROUP)
        qg = jnp.concatenate([q_ref[:, hq * HEAD_DIM:(hq + 1) * HEAD_DIM] for hq in heads], axis=0)
        sink = jnp.concatenate(
            [jnp.broadcast_to(sink_ref[0:1, hq:hq + 1], (1, tq, 1)) for hq in heads], axis=0)
        s = _dot_nt(qg, kk).reshape(GROUP, tq, n_keys) + bias[None]
        m = jnp.maximum(jnp.max(s, axis=-1, keepdims=True), sink)
        e = jnp.exp(s - m)
        den = jnp.sum(e, axis=-1, keepdims=True) + jnp.exp(sink - m)
        o = _dot(e.reshape(GROUP * tq, n_keys).astype(BF16), vv).reshape(GROUP, tq, HEAD_DIM) / den
        for g, hq in enumerate(heads):
            o_ref[:, hq * HEAD_DIM:(hq + 1) * HEAD_DIM] = o[g].astype(BF16)


def _attn_call(q, k, v, sink, batch, seq, n_ctx):
    rows, d = q.shape
    kvw = k.shape[1]
    tq = Q_TILE
    assert seq % tq == 0 and n_ctx % tq == 0 and tq % WINDOW == 0 and (batch * seq) % n_ctx == 0
    nl, nc = seq // tq, n_ctx // tq
    wb = seq // WINDOW
    r = tq // WINDOW

    def q_map(b, i):
        return (jnp.where(i < nl, b * nl + i, batch * nl + b * nc + (i - nl)), 0)

    def prev_map(b, i):
        return (b * wb + jnp.clip(i * r - 1, 0, wb - 1), 0)

    def cur_map(b, i):
        return (b * nl + jnp.minimum(i, nl - 1), 0)

    def next_map(b, i):
        return (b * wb + jnp.minimum((i + 1) * r, wb - 1), 0)

    def ctx_map(b, i):
        return ((batch * seq) // n_ctx + b, 0)

    win = lambda m: pl.BlockSpec((WINDOW, kvw), m)
    cur = pl.BlockSpec((tq, kvw), cur_map)
    ctxs = pl.BlockSpec((n_ctx, kvw), ctx_map)
    return pl.pallas_call(
        functools.partial(_attn_kernel, seq, nl),
        out_shape=jax.ShapeDtypeStruct((rows, d), BF16),
        grid=(batch, nl + nc),
        in_specs=[pl.BlockSpec((tq, d), q_map),
                  win(prev_map), cur, win(next_map), win(prev_map), cur, win(next_map),
                  ctxs, ctxs, _full(sink.shape)],
        out_specs=pl.BlockSpec((tq, d), q_map),
        compiler_params=_cparams("arbitrary", "arbitrary"),
        name="window_attn",
    )(q, k, k, k, v, v, v, k, v, sink)


def _shift_rows(x, k):
    return pltpu.roll(x, k % x.shape[0], 0)


def _lru_kernel(n_ctx_chunks, n_lat_chunks,
                pf_ref, cf_ref, nf_ref, pb_ref, cb_ref, nb_ref,
                cw_ref, cbias_ref, wa_ref, ba_ref, wx_ref, bx_ref, lam_ref,
                hf_ref, hb_ref, carry_ref, a_ref, b_ref, p_ref, h_ref):
    j = pl.program_id(1)
    tc, d = cf_ref.shape
    halo = pf_ref.shape[0]
    nblk = wa_ref.shape[1]
    bw = d // nblk
    is_ctx = j < n_ctx_chunks
    jf = jnp.where(is_ctx, j, j - n_ctx_chunks)
    nseq = jnp.where(is_ctx, n_ctx_chunks, n_lat_chunks)
    jb = nseq - 1 - jf

    @pl.when(j == 0)
    def _():
        carry_ref[...] = jnp.zeros(carry_ref.shape, F32)

    sub = lax.broadcasted_iota(I32, (SUBLANES, LANES), 0)
    run = tc // SUBLANES
    stride = a_ref.shape[1] // SUBLANES
    row_blk = lax.broadcasted_iota(I32, (tc, bw), 0)

    def coeffs(direction, prev_ref, c_ref, n_ref, jj):
        prev = prev_ref[...] * (jj > 0).astype(F32)
        nxt = n_ref[...] * (jj < nseq - 1).astype(F32)
        ext = jnp.concatenate([prev, c_ref[...], nxt], axis=0)
        xr = jnp.zeros((tc, d), F32) + cbias_ref[...]
        for k in range(cw_ref.shape[0]):
            xr = xr + cw_ref[k:k + 1, :] * _shift_rows(ext, 2 - k)[halo:halo + tc]
        for n in range(nblk):
            cs = slice(n * bw, (n + 1) * bw)
            xb = xr[:, cs].astype(BF16)
            r = jax.nn.sigmoid(_dot(xb, wa_ref[direction, n]) + ba_ref[direction:direction + 1, cs])
            gi = jax.nn.sigmoid(_dot(xb, wx_ref[direction, n]) + bx_ref[direction:direction + 1, cs])
            log_a = -LRU_C * r * jax.nn.softplus(-lam_ref[direction:direction + 1, cs])
            a = jnp.exp(log_a)
            mult = jnp.sqrt(1.0 - a * a)
            at_edge = is_ctx & (jj == (0 if direction == 0 else nseq - 1))
            reset_row = jnp.where(at_edge, 0 if direction == 0 else tc - 1, -1)
            mult = jnp.where(row_blk == reset_row, 1.0, mult)
            bb = mult * gi * xr[:, cs]
            for c in range(bw // LANES):
                for r in range(SUBLANES):
                    src = slice(r * run, (r + 1) * run)
                    dst = slice(r * stride, r * stride + run)
                    a_ref[n * (bw // LANES) + c, dst, :] = a[src, c * LANES:(c + 1) * LANES]
                    b_ref[n * (bw // LANES) + c, dst, :] = bb[src, c * LANES:(c + 1) * LANES]

    def scan(direction, out_ref):
        fwd = direction == 0
        for c in range(d // LANES):
            h = p = None
            for i in (range(run) if fwd else range(run - 1, -1, -1)):
                rows = pl.ds(i, SUBLANES, stride=stride)
                a_i, b_i = a_ref[c, rows, :], b_ref[c, rows, :]
                h, p = (b_i, a_i) if h is None else (a_i * h + b_i, a_i * p)
                h_ref[c, rows, :] = h
                p_ref[c, i * SUBLANES:(i + 1) * SUBLANES, :] = p
            for k in (1, 2, 4):
                keep = (sub >= k) if fwd else (sub < SUBLANES - k)
                p_sh, h_sh = _shift_rows(p, k if fwd else -k), _shift_rows(h, k if fwd else -k)
                h = jnp.where(keep, p * h_sh + h, h)
                p = jnp.where(keep, p * p_sh, p)
            lanes = slice(c * LANES, (c + 1) * LANES)
            carry = carry_ref[direction:direction + 1, lanes]
            end_state = p * carry + h
            incoming = jnp.where((sub >= 1) if fwd else (sub < SUBLANES - 1),
                                 _shift_rows(end_state, 1 if fwd else -1), carry)
            for i in range(run):
                rows = pl.ds(i, SUBLANES, stride=stride)
                h_ref[c, rows, :] = h_ref[c, rows, :] + p_ref[c, i * SUBLANES:(i + 1) * SUBLANES, :] * incoming
            for r in range(SUBLANES):
                out_ref[r * run:(r + 1) * run, lanes] = h_ref[c, r * stride:r * stride + run, :]
            carry_ref[direction:direction + 1, lanes] = (
                end_state[SUBLANES - 1:SUBLANES] if fwd else end_state[0:1])

    coeffs(0, pf_ref, cf_ref, nf_ref, jf)
    scan(0, hf_ref)
    coeffs(1, pb_ref, cb_ref, nb_ref, jb)
    scan(1, hb_ref)


def _lru_call(rec, conv_w, conv_b, wa, ba, wx, bx, lam, batch, seq, n_ctx):
    rows, d = rec.shape
    tc = SEQ_CHUNK
    halo = SUBLANES
    assert seq % tc == 0 and n_ctx % tc == 0
    ncc, nlc = n_ctx // tc, seq // tc
    hb_per_chunk = tc // halo
    last_halo = rows // halo - 1

    def chunk_f(b, j):
        return jnp.where(j < ncc, (batch * seq) // tc + b * ncc + j, b * nlc + (j - ncc))

    def chunk_b(b, j):
        return jnp.where(j < ncc, (batch * seq) // tc + b * ncc + (ncc - 1 - j),
                         b * nlc + (nlc - 1 - (j - ncc)))

    def cur(f):
        return pl.BlockSpec((tc, d), lambda b, j: (f(b, j), 0))

    def prev(f):
        return pl.BlockSpec((halo, d), lambda b, j: (jnp.maximum(f(b, j) * hb_per_chunk - 1, 0), 0))

    def nxt(f):
        return pl.BlockSpec((halo, d),
                            lambda b, j: (jnp.minimum((f(b, j) + 1) * hb_per_chunk, last_halo), 0))

    out = jax.ShapeDtypeStruct((rows, d), F32)
    return pl.pallas_call(
        functools.partial(_lru_kernel, ncc, nlc),
        out_shape=(out, out),
        grid=(batch, ncc + nlc),
        in_specs=[prev(chunk_f), cur(chunk_f), nxt(chunk_f), prev(chunk_b), cur(chunk_b), nxt(chunk_b),
                  _full(conv_w.shape), _full(conv_b.shape), _full(wa.shape), _full(ba.shape),
                  _full(wx.shape), _full(bx.shape), _full(lam.shape)],
        out_specs=(cur(chunk_f), cur(chunk_b)),
        scratch_shapes=[pltpu.VMEM((2, d), F32)]
        + [pltpu.VMEM((d // LANES, tc + SUBLANES * SCAN_RUN_PAD, LANES), F32)] * 2
        + [pltpu.VMEM((d // LANES, tc, LANES), F32),
           pltpu.VMEM((d // LANES, tc + SUBLANES * SCAN_RUN_PAD, LANES), F32)],
        compiler_params=_cparams("arbitrary", "arbitrary"),
        name="rglru_scan",
    )(rec, rec, rec, rec, rec, rec, conv_w, conv_b, wa, ba, wx, bx, lam)


def _dwconv_kernel(n_lat_chunks_total, lat_chunks, ctx_chunks,
                   p_ref, c_ref, n_ref, w_ref, b_ref, g_ref, beta_ref, o_ref):
    ci = pl.program_id(0)
    tc, d = c_ref.shape
    halo = p_ref.shape[0]
    ktaps = w_ref.shape[0]
    is_lat = ci < n_lat_chunks_total
    pos = jnp.where(is_lat, ci % lat_chunks, (ci - n_lat_chunks_total) % ctx_chunks)
    nseq = jnp.where(is_lat, lat_chunks, ctx_chunks)
    prev = p_ref[...] * (pos > 0).astype(F32)
    nxt = n_ref[...] * (pos < nseq - 1).astype(F32)
    ext = jnp.concatenate([prev, c_ref[...], nxt], axis=0)
    acc = jnp.zeros((tc, d), F32) + b_ref[...]
    rotated = [ext if r == 0 else _shift_rows(ext, r) for r in range(SUBLANES)]
    for k in range(ktaps):
        a, r = divmod(ktaps // 2 - k, SUBLANES)
        acc = acc + w_ref[k:k + 1, :] * rotated[r][halo - SUBLANES * a:halo - SUBLANES * a + tc]
    z = _layer_norm(acc, g_ref[...], beta_ref[...])
    o_ref[...] = (z * jax.nn.sigmoid(z)).astype(BF16)


def _dwconv_call(glu, w, b, g, beta, batch, seq, n_ctx):
    rows, d = glu.shape
    tc = SEQ_CHUNK
    halo = 2 * SUBLANES
    assert w.shape[0] // 2 <= halo and seq % tc == 0 and n_ctx % tc == 0
    per = tc // halo
    last_halo = rows // halo - 1
    return pl.pallas_call(
        functools.partial(_dwconv_kernel, batch * seq // tc, seq // tc, n_ctx // tc),
        out_shape=jax.ShapeDtypeStruct((rows, d), BF16),
        grid=(rows // tc,),
        in_specs=[pl.BlockSpec((halo, d), lambda i: (jnp.maximum(i * per - 1, 0), 0)),
                  pl.BlockSpec((tc, d), lambda i: (i, 0)),
                  pl.BlockSpec((halo, d), lambda i: (jnp.minimum((i + 1) * per, last_halo), 0)),
                  _full(w.shape), _full(b.shape), _full(g.shape), _full(beta.shape)],
        out_specs=pl.BlockSpec((tc, d), lambda i: (i, 0)),
        compiler_params=_cparams("arbitrary"),
        name="dwconv_ln_swish",
    )(glu, glu, glu, w, b, g, beta)


def _post_kernel(kind, has_bias, alpha, *refs):
    refs = list(refs)
    if kind == "lru":
        hf_ref, hb_ref, gate_ref = refs[:3]
        refs = refs[3:]
        core = ((hf_ref[...] + hb_ref[...]) * gate_ref[...]).astype(BF16)
    else:
        core = refs[0][...]
        refs = refs[1:]
    w_ref = refs[0]
    refs = refs[1:]
    y = _dot(core, w_ref[...])
    if has_bias:
        y = y + refs[0][...]
        refs = refs[1:]
    x_ref, mod_ref, lng_ref, lnb_ref, rt_ref, x1_ref, t_ref, aff_ref = refs
    m = mod_ref[0]
    x1 = _layer_norm(alpha * x_ref[...] + m[2:3] * y, lng_ref[...], lnb_ref[...])
    x1_ref[...] = x1
    t = (x1 * (1.0 + m[4:5]) + m[3:4]).astype(BF16)
    t_ref[...] = t
    logits = _dot_nt(rt_ref[...], t)
    e = jnp.exp(logits - jnp.max(logits, axis=0, keepdims=True))
    aff_ref[...] = e / jnp.sum(e, axis=0, keepdims=True)


def _post_call(kind, alpha, cores, w, bias, x_all, mod, ln_g, ln_b, router_t, n_rows, tiles_per_sample,
               n_samples):
    d = x_all.shape[1]
    tm = ROW_TILE
    n_e = router_t.shape[0]
    x_spec, mod_spec = _row_specs(tm, d, tiles_per_sample, n_samples)
    in_specs = [x_spec] * len(cores) + [_full(w.shape)]
    args = list(cores) + [w]
    if bias is not None:
        in_specs.append(_full(bias.shape))
        args.append(bias)
    in_specs += [x_spec, mod_spec, _full(ln_g.shape), _full(ln_b.shape), _full(router_t.shape)]
    args += [x_all, mod, ln_g, ln_b, router_t]
    return pl.pallas_call(
        functools.partial(_post_kernel, kind, bias is not None, alpha),
        out_shape=(jax.ShapeDtypeStruct((n_rows, d), F32),
                   jax.ShapeDtypeStruct((n_rows, d), BF16),
                   jax.ShapeDtypeStruct((n_e, n_rows), F32)),
        grid=(n_rows // tm,),
        in_specs=in_specs,
        out_specs=(x_spec, x_spec, pl.BlockSpec((n_e, tm), lambda i: (0, i))),
        compiler_params=_cparams("arbitrary"),
        name="mixer_out_ln_router",
    )(*args)


def _route_kernel(cap, slot_tile, aff_ref, pos_ref, excl_ref, lo_ref, hi_ref):
    n_e, seq = aff_ref.shape
    tb = min(TOKEN_BLOCK, seq)
    nblk = seq // tb
    bits = pltpu.bitcast(aff_ref[...], I32)

    def search(_, lohi):
        lo, hi = lohi
        mid = lo + jnp.right_shift(hi - lo, 1)
        cnt = jnp.sum((bits >= mid).astype(F32), axis=1, keepdims=True)
        ge = cnt >= cap
        return jnp.where(ge, mid, lo), jnp.where(ge, hi, mid)

    lo0 = jnp.zeros((n_e, 1), I32)
    hi0 = jnp.full((n_e, 1), 0x7F800000, I32)
    thr, _ = lax.fori_loop(0, 31, search, (lo0, hi0))
    n_gt = jnp.sum((bits > thr).astype(F32), axis=1, keepdims=True)
    need = cap - n_gt

    upper = (lax.broadcasted_iota(I32, (tb, tb), 0) < lax.broadcasted_iota(I32, (tb, tb), 1)).astype(BF16)
    lane = lax.broadcasted_iota(I32, (n_e, LANES), 1)
    eq_off = jnp.zeros((n_e, 1), F32)
    pos_off = jnp.zeros((n_e, 1), F32)
    excl = jnp.full((n_e, LANES), float(cap), F32)
    incl = jnp.full((n_e, LANES), 1e9, F32)
    for blk in range(nblk):
        cs = slice(blk * tb, (blk + 1) * tb)
        bb = bits[:, cs]
        gt = bb > thr
        eq = bb == thr
        eqf = jnp.where(eq, 1.0, 0.0)
        rank = _dot(eqf.astype(BF16), upper) + eq_off
        sel = gt | (eq & (rank < need))
        self_f = jnp.where(sel, 1.0, 0.0)
        pin = _dot(self_f.astype(BF16), upper) + pos_off
        pos_ref[:, cs] = jnp.where(sel, pin, -1.0)
        excl = jnp.where(lane == blk, pos_off, excl)
        eq_off = eq_off + jnp.sum(eqf, axis=1, keepdims=True)
        pos_off = pos_off + jnp.sum(self_f, axis=1, keepdims=True)
        incl = jnp.where(lane == blk, pos_off, incl)
    excl_ref[0] = excl.astype(I32)
    lo = jnp.zeros((n_e, LANES), F32)
    hi = jnp.zeros((n_e, LANES), F32)
    for q in range(cap // slot_tile):
        first = float(q * slot_tile)
        last = float(q * slot_tile + slot_tile - 1)
        lo_q = jnp.sum(jnp.where(incl <= first, 1.0, 0.0), axis=1, keepdims=True)
        hi_q = jnp.sum(jnp.where(incl <= last, 1.0, 0.0), axis=1, keepdims=True)
        lo = jnp.where(lane == q, lo_q, lo)
        hi = jnp.where(lane == q, jnp.minimum(hi_q, float(nblk - 1)), hi)
    lo_ref[0] = jnp.broadcast_to(jnp.min(lo, axis=0, keepdims=True), lo.shape).astype(I32)
    hi_ref[0] = jnp.broadcast_to(jnp.max(hi, axis=0, keepdims=True), hi.shape).astype(I32)


def _route_call(aff_t, batch, seq, first_block, cap, slot_tile):
    n_e = aff_t.shape[0]
    assert seq // min(TOKEN_BLOCK, seq) < LANES and cap // slot_tile <= LANES
    meta = jax.ShapeDtypeStruct((batch, n_e, LANES), I32)
    meta_spec = pl.BlockSpec((1, n_e, LANES), lambda b: (b, 0, 0))
    return pl.pallas_call(
        functools.partial(_route_kernel, cap, slot_tile),
        out_shape=(jax.ShapeDtypeStruct((n_e, batch * seq), F32), meta, meta, meta),
        grid=(batch,),
        in_specs=[pl.BlockSpec((n_e, seq), lambda b: (0, first_block + b))],
        out_specs=(pl.BlockSpec((n_e, seq), lambda b: (0, b)), meta_spec, meta_spec, meta_spec),
        compiler_params=_cparams("arbitrary"),
        name="expert_choice_route",
    )(aff_t)


def _dispatch_kernel(lo_ref, hi_ref, t_ref, pos_ref, aff_ref, xg_ref, gate_ref, acc_ref):
    b = pl.program_id(0)
    q = pl.program_id(1)
    n_e, slot_tile, d = xg_ref.shape[1], xg_ref.shape[2], xg_ref.shape[3]
    tb = pos_ref.shape[3]
    step = b * pl.num_programs(1) + q
    slot = (lax.broadcasted_iota(I32, (slot_tile, tb), 0) + q * slot_tile).astype(F32)
    acc_ref[...] = jnp.zeros(acc_ref.shape, F32)

    def body(blk, gacc):
        hits = [pos_ref[e, blk] == slot for e in range(n_e)]
        onehot = jnp.concatenate([jnp.where(h, 1.0, 0.0).astype(BF16) for h in hits], axis=0)
        x_blk = t_ref[pl.ds(pl.multiple_of(blk * tb, tb), tb), :]
        acc_ref[...] += _dot(onehot, x_blk)
        gates = [jnp.sum(jnp.where(h, aff_ref[e, blk], 0.0), axis=1, keepdims=True)
                 for e, h in enumerate(hits)]
        return gacc + jnp.concatenate(gates, axis=0)

    gacc = lax.fori_loop(lo_ref[step], hi_ref[step] + 1, body, jnp.zeros((n_e * slot_tile, 1), F32))
    for e in range(n_e):
        rows = slice(e * slot_tile, (e + 1) * slot_tile)
        xg_ref[0, e] = acc_ref[rows, :].astype(BF16)
        gate_ref[0, e] = gacc[rows]


def _dispatch_call(t_all, pos, aff_t, lo, hi, batch, seq, first_seq_block, aff_first_block, cap, slot_tile):
    d = t_all.shape[1]
    n_e = pos.shape[0]
    tb = min(TOKEN_BLOCK, seq)
    nblk = seq // tb
    pos4 = pos.reshape(n_e, batch * nblk, 1, tb)
    aff4 = aff_t.reshape(n_e, aff_t.shape[1] // tb, 1, tb)
    grid_spec = pltpu.PrefetchScalarGridSpec(
        num_scalar_prefetch=2,
        grid=(batch, cap // slot_tile),
        in_specs=[
            pl.BlockSpec((seq, d), lambda b, q, *_: (first_seq_block + b, 0), pipeline_mode=pl.Buffered(1)),
            pl.BlockSpec((n_e, nblk, 1, tb), lambda b, q, *_: (0, b, 0, 0)),
            pl.BlockSpec((n_e, nblk, 1, tb), lambda b, q, *_: (0, aff_first_block + b, 0, 0)),
        ],
        out_specs=(pl.BlockSpec((1, n_e, slot_tile, d), lambda b, q, *_: (b, 0, q, 0)),
                   pl.BlockSpec((1, n_e, slot_tile, 1), lambda b, q, *_: (b, 0, q, 0))),
        scratch_shapes=[pltpu.VMEM((n_e * slot_tile, d), F32)],
    )
    return pl.pallas_call(
        _dispatch_kernel,
        out_shape=(jax.ShapeDtypeStruct((batch, n_e, cap, d), BF16),
                   jax.ShapeDtypeStruct((batch, n_e, cap, 1), F32)),
        grid_spec=grid_spec,
        compiler_params=_cparams("arbitrary", "arbitrary"),
        name="moe_dispatch",
    )(lo.reshape(-1), hi.reshape(-1), t_all, pos4, aff4)


def _ffn_kernel(n_streams, w1_ref, w3_ref, w2_ref, *refs):
    ins, outs, accs = refs[:2 * n_streams], refs[2 * n_streams:3 * n_streams], refs[3 * n_streams:]
    f = pl.program_id(2)
    w1 = w1_ref[0, 0].astype(BF16)
    w3 = w3_ref[0, 0].astype(BF16)
    w2 = w2_ref[0, 0].astype(BF16)
    for s in range(n_streams):
        xg_ref, gate_ref, y_ref, acc_ref = ins[2 * s], ins[2 * s + 1], outs[s], accs[s]
        cap = acc_ref.shape[0]
        rt = min(FFN_ROW_TILE, cap)

        @pl.when(f == 0)
        def _(acc_ref=acc_ref):
            acc_ref[...] = jnp.zeros(acc_ref.shape, F32)

        for r in range(cap // rt):
            rows = slice(r * rt, (r + 1) * rt)
            xg = xg_ref[0, 0, rows, :]
            h1 = _dot(xg, w1)
            h3 = _dot(xg, w3)
            hid = (h1 * jax.nn.sigmoid(h1) * h3).astype(BF16)
            acc_ref[rows, :] += _dot(hid, w2)

        @pl.when(f == pl.num_programs(2) - 1)
        def _(acc_ref=acc_ref, gate_ref=gate_ref, y_ref=y_ref):
            y_ref[0, 0] = (acc_ref[...] * gate_ref[0, 0]).astype(BF16)


def _ffn_call(layer, streams, w1, w3, w2):
    n_e, d, ff = w1.shape[1:]
    batch = streams[0][0].shape[0]
    tf = FF_TILE
    tok = lambda e, b, f: (b, e, 0, 0)
    in_specs = [pl.BlockSpec((1, 1, d, tf), lambda e, b, f: (layer, e, 0, f)),
                pl.BlockSpec((1, 1, d, tf), lambda e, b, f: (layer, e, 0, f)),
                pl.BlockSpec((1, 1, tf, d), lambda e, b, f: (layer, e, f, 0))]
    args, out_shape, out_specs, scratch = [w1, w3, w2], [], [], []
    for xg, gate in streams:
        cap = xg.shape[2]
        assert cap % min(FFN_ROW_TILE, cap) == 0
        in_specs += [pl.BlockSpec((1, 1, cap, d), tok), pl.BlockSpec((1, 1, cap, 1), tok)]
        args += [xg, gate]
        out_shape.append(jax.ShapeDtypeStruct(xg.shape, BF16))
        out_specs.append(pl.BlockSpec((1, 1, cap, d), tok))
        scratch.append(pltpu.VMEM((cap, d), F32))
    return pl.pallas_call(
        functools.partial(_ffn_kernel, len(streams)),
        out_shape=tuple(out_shape),
        grid=(n_e, batch, ff // tf),
        in_specs=in_specs,
        out_specs=tuple(out_specs),
        scratch_shapes=scratch,
        compiler_params=_cparams("arbitrary", "arbitrary", "arbitrary"),
        name="expert_ffn",
    )(*args)


def _combine_kernel(n_e, group, nblk, has_alias, alpha, excl_ref, *refs):
    if has_alias:
        refs = refs[1:]
    (x1_ref, pos_ref, mod_ref, lng_ref, lnb_ref, y_hbm, o_ref,
     buf_ref, sem_ref, base_ref, exp_ref, count_ref) = refs
    b = pl.program_id(0)
    i = pl.program_id(1)
    n_b = pl.num_programs(0)
    tb, d = x1_ref.shape
    per_tile = MXU_DEPTH // group
    max_groups = base_ref.shape[0] // 2

    def group_copy(bb, e, base, par, g):
        return pltpu.make_async_copy(
            y_hbm.at[bb, e, pl.ds(pl.multiple_of(base, group), group)],
            buf_ref.at[par, pl.ds(pl.multiple_of(g * group, group), group)], sem_ref.at[par])

    def issue_tile(bb, ii, par):
        n_groups = jnp.int32(0)
        for e in range(n_e):
            first = excl_ref[(bb * n_e + e) * LANES + ii]
            end = excl_ref[(bb * n_e + e) * LANES + ii + 1]
            start = (first // group) * group
            n_run = jnp.where(end > first, (end - start + group - 1) // group, 0)

            def issue(k, g, e=e, start=start):
                base = start + k * group
                base_ref[par * max_groups + g] = base
                exp_ref[par * max_groups + g] = e
                group_copy(bb, e, base, par, g).start()
                return g + 1

            n_groups = lax.fori_loop(0, n_run, issue, n_groups)
        count_ref[par] = n_groups

    @pl.when(i < nblk)
    def _():
        tile_idx = b * nblk + i
        par = tile_idx % 2

        @pl.when(tile_idx == 0)
        def _():
            buf_ref[...] = jnp.zeros(buf_ref.shape, BF16)
            issue_tile(b, i, par)

        more_here = i + 1 < nblk

        @pl.when(more_here | (b + 1 < n_b))
        def _():
            issue_tile(jnp.where(more_here, b, b + 1), jnp.where(more_here, i + 1, 0), 1 - par)

        n_groups = count_ref[par]

        def wait_one(_, carry):
            group_copy(0, 0, 0, par, 0).wait()
            return carry

        lax.fori_loop(0, n_groups, wait_one, 0)
        row = lax.broadcasted_iota(I32, (group, tb), 0).astype(F32)

        def tile(kt, f):
            parts = []
            for j in range(per_tile):
                g = kt * per_tile + j
                live = g < n_groups
                gc = par * max_groups + jnp.minimum(g, n_groups - 1)
                base = jnp.where(live, base_ref[gc], -2 * group).astype(F32)
                hit = (pos_ref[exp_ref[gc]] - base) == row
                parts.append(jnp.where(hit, 1.0, 0.0).astype(BF16))
            onehot = jnp.concatenate(parts, axis=0)
            rows = buf_ref[par, pl.ds(pl.multiple_of(kt * MXU_DEPTH, MXU_DEPTH), MXU_DEPTH), :]
            return f + _dot_tn(onehot, rows)

        n_tiles = (n_groups + per_tile - 1) // per_tile
        f = lax.fori_loop(0, n_tiles, tile, jnp.zeros((tb, d), F32))
        m = mod_ref[0]
        o_ref[...] = _layer_norm(alpha * x1_ref[...] + m[5:6] * f, lng_ref[...], lnb_ref[...])

    @pl.when(i >= nblk)
    def _():
        o_ref[...] = x1_ref[...]


def _combine_call(alpha, x1, pos, excl, y, mod, ln_g, ln_b, batch, seq, x_first_block, mod_row, out_rows,
                  alias_buf, carry_first_block=0, carry_blocks=0):
    d = x1.shape[1]
    n_e, cap = y.shape[1], y.shape[2]
    tb = min(TOKEN_BLOCK, seq)
    nblk = seq // tb
    group = COMBINE_GROUP
    assert cap % group == 0 and MXU_DEPTH % group == 0 and group % BF16_ROWS == 0
    max_groups = n_e * min(tb // group + 1, cap // group)
    buf_rows = -(-max_groups * group // MXU_DEPTH) * MXU_DEPTH
    has_alias = alias_buf is not None

    def row_map(b, i, *_):
        return (jnp.where(i < nblk, x_first_block + b * nblk + i,
                          carry_first_block + b * carry_blocks + (i - nblk)), 0)

    mod_map = (lambda b, i, *_: (b, 0, 0)) if mod_row is None else (lambda b, i, *_: (mod_row, 0, 0))
    in_specs = [pl.BlockSpec((tb, d), row_map),
                pl.BlockSpec((n_e, 1, tb), lambda b, i, *_: (0, 0, b * nblk + jnp.minimum(i, nblk - 1))),
                pl.BlockSpec((1, 6, d), mod_map),
                _full(ln_g.shape), _full(ln_b.shape),
                pl.BlockSpec(memory_space=pl.ANY)]
    args = [x1, pos.reshape(n_e, 1, -1), mod, ln_g, ln_b, y]
    aliases = {}
    if has_alias:
        in_specs = [pl.BlockSpec(memory_space=pl.ANY)] + in_specs
        args = [alias_buf] + args
        aliases = {1: 0}
    grid_spec = pltpu.PrefetchScalarGridSpec(
        num_scalar_prefetch=1,
        grid=(batch, nblk + carry_blocks),
        in_specs=in_specs,
        out_specs=pl.BlockSpec((tb, d), row_map),
        scratch_shapes=[pltpu.VMEM((2, buf_rows, d), BF16), pltpu.SemaphoreType.DMA((2,)),
                        pltpu.SMEM((2 * max_groups,), I32), pltpu.SMEM((2 * max_groups,), I32),
                        pltpu.SMEM((2,), I32)],
    )
    return pl.pallas_call(
        functools.partial(_combine_kernel, n_e, group, nblk, has_alias, alpha),
        out_shape=jax.ShapeDtypeStruct((out_rows, d), F32),
        grid_spec=grid_spec,
        input_output_aliases=aliases,
        compiler_params=_cparams("arbitrary", "arbitrary"),
        name="moe_combine_ln",
    )(excl.reshape(-1), *args)


def _rope_tables(seq, pad_rows):
    rows = seq // GRID_W
    row = jnp.broadcast_to(jnp.arange(rows, dtype=F32)[:, None], (rows, GRID_W)).reshape(-1)
    col = jnp.broadcast_to(jnp.arange(GRID_W, dtype=F32)[None, :], (rows, GRID_W)).reshape(-1)
    n_freq = HEAD_DIM // 4
    inv = ROPE_BASE ** (-jnp.arange(n_freq, dtype=F32) / n_freq)
    ang = jnp.concatenate([row[:, None] * inv, col[:, None] * inv], axis=-1)
    cos, sin = jnp.cos(ang), jnp.sin(ang)
    reps = LANES // HEAD_DIM
    cos_t = jnp.tile(jnp.concatenate([cos, cos], axis=-1), (1, reps))
    sin_t = jnp.tile(jnp.concatenate([-sin, sin], axis=-1), (1, reps))
    cos_t = jnp.concatenate([cos_t, jnp.ones((pad_rows, LANES), F32)], axis=0)
    sin_t = jnp.concatenate([sin_t, jnp.zeros((pad_rows, LANES), F32)], axis=0)
    return cos_t, sin_t


def _moe(alpha, layer, x1, t, aff_t, mod, ln_g, ln_b, w1, w3, w2, batch, seq, n_ctx, with_ctx, last):
    n_e = aff_t.shape[0]
    n_lat = batch * seq
    cap = max(1, EC_CAPACITY_FACTOR * seq // n_e)
    st = min(SLOT_TILE, cap)
    pos, excl, lo, hi = _route_call(aff_t, batch, seq, 0, cap, st)
    streams = [_dispatch_call(t, pos, aff_t, lo[:, 0, :cap // st], hi[:, 0, :cap // st],
                              batch, seq, 0, 0, cap, st)]
    if with_ctx:
        capc = max(1, EC_CAPACITY_FACTOR * n_ctx // n_e)
        stc = min(SLOT_TILE, capc)
        tbc = min(TOKEN_BLOCK, n_ctx)
        posc, exclc, loc, hic = _route_call(aff_t, batch, n_ctx, n_lat // n_ctx, capc, stc)
        streams.append(_dispatch_call(t, posc, aff_t, loc[:, 0, :capc // stc], hic[:, 0, :capc // stc],
                                      batch, n_ctx, n_lat // n_ctx, n_lat // tbc, capc, stc))
    ys = _ffn_call(layer, streams, w1, w3, w2)
    out_rows = n_lat if last else x1.shape[0]
    tb = min(TOKEN_BLOCK, seq)
    assert last or n_ctx % tb == 0
    x2 = _combine_call(alpha, x1, pos, excl, ys[0], mod, ln_g, ln_b, batch, seq, 0, None, out_rows, None,
                       carry_first_block=n_lat // tb, carry_blocks=0 if last else n_ctx // tb)
    if with_ctx:
        x2 = _combine_call(alpha, x1, posc, exclc, ys[1], mod, ln_g, ln_b, batch, n_ctx, n_lat // tbc, batch,
                           x1.shape[0], x2)
    return x2


def kernel(x, c, ctx, c_ctx, ada_w, ada_b, ln1_g, ln1_b, ln2_g, ln2_b, attn_w_qkv, attn_sink, attn_w_o, lru_w_in, lru_conv_w, lru_conv_b, lru_gate_a_w, lru_gate_a_b, lru_gate_x_w, lru_gate_x_b, lru_lambda, lru_w_out, conv_w_in, conv_b_in, conv_dw_w, conv_dw_b, conv_ln_g, conv_ln_b, conv_w_out, conv_b_out, moe_router, moe_w1, moe_w3, moe_w2):
    batch, seq, d = x.shape
    n_ctx = ctx.shape[1]
    depth = ada_w.shape[0]
    n_lat = batch * seq
    rows = n_lat + batch * n_ctx
    tm = ROW_TILE
    assert seq % tm == 0 and (batch * n_ctx) % tm == 0
    tiles_per_sample = seq // tm
    alpha = float((2 * depth) ** 0.25)

    x_all = jnp.concatenate([x.reshape(n_lat, d), ctx.reshape(batch * n_ctx, d)], axis=0)
    cond = jnp.zeros((SUBLANES, d), F32).at[:batch].set(c).at[batch].set(c_ctx)
    mods = _ada_call(cond, ada_w, ada_b)[:, :batch + 1].reshape(depth, batch + 1, 6, d)
    cos_t, sin_t = _rope_tables(seq, tm)
    x_spec, mod_spec = _row_specs(tm, d, tiles_per_sample, batch)
    vec = lambda a: a.reshape(1, -1)

    for i in range(depth):
        last = i == depth - 1
        kind, j = i % 3, i // 3
        mod = mods[i]
        n_rows = rows
        if kind == 0:
            w_qkv = attn_w_qkv[j].astype(BF16)
            kvw = N_KV_HEADS * HEAD_DIM
            rope_spec = pl.BlockSpec(
                (tm, LANES), lambda t: (jnp.where(t < batch * tiles_per_sample, t % tiles_per_sample,
                                                  tiles_per_sample), 0))
            q, k, v = pl.pallas_call(
                _attn_in_kernel,
                out_shape=(jax.ShapeDtypeStruct((rows, d), BF16),
                           jax.ShapeDtypeStruct((rows, kvw), BF16),
                           jax.ShapeDtypeStruct((rows, kvw), BF16)),
                grid=(rows // tm,),
                in_specs=[x_spec, mod_spec, _full(w_qkv.shape), rope_spec, rope_spec],
                out_specs=(x_spec, pl.BlockSpec((tm, kvw), lambda t: (t, 0)),
                           pl.BlockSpec((tm, kvw), lambda t: (t, 0))),
                compiler_params=_cparams("arbitrary"),
                name="attn_in_proj",
            )(x_all, mod, w_qkv, cos_t, sin_t)
            o = _attn_call(q, k, v, vec(attn_sink[j]), batch, seq, n_ctx)
            cores, w_out, b_out, pkind = [o], attn_w_o[j].astype(BF16), None, "direct"
        elif kind == 1:
            w_in = lru_w_in[j].astype(BF16)
            gate, rec = pl.pallas_call(
                _lru_in_kernel,
                out_shape=(jax.ShapeDtypeStruct((rows, d), F32), jax.ShapeDtypeStruct((rows, d), F32)),
                grid=(rows // tm,),
                in_specs=[x_spec, mod_spec, _full(w_in.shape)],
                out_specs=(x_spec, x_spec),
                compiler_params=_cparams("arbitrary"),
                name="lru_in_proj",
            )(x_all, mod, w_in)
            hf, hb = _lru_call(rec, lru_conv_w[j], vec(lru_conv_b[j]),
                               lru_gate_a_w[j].astype(BF16), lru_gate_a_b[j],
                               lru_gate_x_w[j].astype(BF16), lru_gate_x_b[j], lru_lambda[j],
                               batch, seq, n_ctx)
            cores, w_out, b_out, pkind = [hf, hb, gate], lru_w_out[j].astype(BF16), None, "lru"
        else:
            w_in = conv_w_in[j].astype(BF16)
            glu = pl.pallas_call(
                _conv_in_kernel,
                out_shape=jax.ShapeDtypeStruct((rows, d), F32),
                grid=(rows // tm,),
                in_specs=[x_spec, mod_spec, _full(w_in.shape), _full((1, 2 * d))],
                out_specs=x_spec,
                compiler_params=_cparams("arbitrary"),
                name="conv_in_proj",
            )(x_all, mod, w_in, vec(conv_b_in[j]))
            core = _dwconv_call(glu, conv_dw_w[j], vec(conv_dw_b[j]), vec(conv_ln_g[j]), vec(conv_ln_b[j]),
                                batch, seq, n_ctx)
            cores, w_out, b_out, pkind = [core], conv_w_out[j].astype(BF16), vec(conv_b_out[j]), "direct"
        post_rows = n_lat if last else rows
        x1, t, aff_t = _post_call(pkind, alpha, cores, w_out, b_out, x_all, mod, vec(ln1_g[i]), vec(ln1_b[i]),
                                  moe_router[i].T.astype(BF16), post_rows, tiles_per_sample, batch)
        x_all = _moe(alpha, i, x1, t, aff_t, mod, vec(ln2_g[i]), vec(ln2_b[i]), moe_w1, moe_w3, moe_w2,
                     batch, seq, n_ctx, not last, last)
    return x_all.reshape(batch, seq, d)
```

```python
import functools

import jax
import jax.numpy as jnp
from jax import lax
from jax.experimental import pallas as pl
from jax.experimental.pallas import tpu as pltpu

F32 = jnp.float32
BF16 = jnp.bfloat16
I32 = jnp.int32

HEAD_DIM = 64
N_KV_HEADS = 4
GROUP = 4
WINDOW = 128
GRID_W = 64
ROPE_BASE = 10000.0
N_EXPERTS = 16
EC_CAPACITY_FACTOR = 2
LRU_C = 8.0
LN_EPS = 1e-5
NEG_INF = -1e30

ROW_TILE = 512
Q_TILE = 128
SEQ_CHUNK = 256
SCAN_RUN_PAD = 8
TOKEN_BLOCK = 256
COMBINE_TOKENS = 512
SLOT_TILE = 64
FF_TILE = 512
FFN_ROW_TILE = 512
COMBINE_GROUP = 32
MXU_DEPTH = 256
COMBINE_TILE_ROWS = 1024
LANES = 128
SUBLANES = 8
BF16_ROWS = 16
VMEM_LIMIT = 56 * 1024 * 1024


def _cparams(*sem):
    return pltpu.CompilerParams(dimension_semantics=sem, vmem_limit_bytes=VMEM_LIMIT)


def _layer_norm(y, g, b):
    mu = jnp.mean(y, axis=-1, keepdims=True)
    d = y - mu
    var = jnp.mean(d * d, axis=-1, keepdims=True)
    return d * lax.rsqrt(var + LN_EPS) * g + b


def _dot(a, b):
    return jnp.dot(a, b, preferred_element_type=F32)


def _dot_nt(a, b):
    return lax.dot_general(a, b, (((1,), (1,)), ((), ())), preferred_element_type=F32)


def _dot_tn(a, b):
    return lax.dot_general(a, b, (((0,), (0,)), ((), ())), preferred_element_type=F32)


def _ada_kernel(c_ref, w_ref, b_ref, o_ref):
    cnd = c_ref[...]
    s = cnd * jax.nn.sigmoid(cnd)
    o_ref[0] = _dot(s.astype(BF16), w_ref[0].astype(BF16)) + b_ref[0]


def _ada_call(cond, ada_w, ada_b):
    depth, d, n6 = ada_w.shape
    rows = cond.shape[0]
    tn = 1536
    return pl.pallas_call(
        _ada_kernel,
        out_shape=jax.ShapeDtypeStruct((depth, rows, n6), F32),
        grid=(depth, n6 // tn),
        in_specs=[
            pl.BlockSpec((rows, d), lambda l, n: (0, 0)),
            pl.BlockSpec((1, d, tn), lambda l, n: (l, 0, n)),
            pl.BlockSpec((1, 1, tn), lambda l, n: (l, 0, n)),
        ],
        out_specs=pl.BlockSpec((1, rows, tn), lambda l, n: (l, 0, n)),
        compiler_params=_cparams("arbitrary", "arbitrary"),
        name="ada_mod",
    )(cond, ada_w, ada_b.reshape(depth, 1, n6))


def _modulate1(x_ref, mod_ref):
    m = mod_ref[0]
    return (x_ref[...] * (1.0 + m[1:2]) + m[0:1]).astype(BF16)


def _attn_in_kernel(x_ref, mod_ref, w_ref, cos_ref, sin_ref, q_ref, k_ref, v_ref):
    d = x_ref.shape[1]
    z = _dot(_modulate1(x_ref, mod_ref), w_ref[...])
    cos = cos_ref[...]
    sin = sin_ref[...]
    lane = lax.broadcasted_iota(I32, cos.shape, 1)
    first_half = (lane % HEAD_DIM) < (HEAD_DIM // 2)

    def rope(t):
        swapped = jnp.where(first_half, pltpu.roll(t, LANES - HEAD_DIM // 2, 1),
                            pltpu.roll(t, HEAD_DIM // 2, 1))
        return t * cos + swapped * sin

    scale = HEAD_DIM ** -0.5
    for cidx in range(d // LANES):
        sl = slice(cidx * LANES, (cidx + 1) * LANES)
        q_ref[:, sl] = (rope(z[:, sl]) * scale).astype(BF16)
    kvw = k_ref.shape[1]
    for cidx in range(kvw // LANES):
        sl = slice(cidx * LANES, (cidx + 1) * LANES)
        k_ref[:, sl] = rope(z[:, d + cidx * LANES:d + (cidx + 1) * LANES]).astype(BF16)
    v_ref[...] = z[:, d + kvw:].astype(BF16)


def _lru_in_kernel(x_ref, mod_ref, w_ref, g_ref, rec_ref):
    d = x_ref.shape[1]
    z = _dot(_modulate1(x_ref, mod_ref), w_ref[...])
    g_ref[...] = jax.nn.gelu(z[:, :d])
    rec_ref[...] = z[:, d:]


def _conv_in_kernel(x_ref, mod_ref, w_ref, b_ref, o_ref):
    d = x_ref.shape[1]
    z = _dot(_modulate1(x_ref, mod_ref), w_ref[...]) + b_ref[...]
    o_ref[...] = z[:, :d] * jax.nn.sigmoid(z[:, d:])


def _row_specs(tm, d, tiles_per_sample, n_samples):
    x_spec = pl.BlockSpec((tm, d), lambda i: (i, 0))
    mod_spec = pl.BlockSpec((1, 6, d), lambda i: (jnp.minimum(i // tiles_per_sample, n_samples), 0, 0))
    return x_spec, mod_spec


def _full(shape):
    n = len(shape)
    return pl.BlockSpec(shape, lambda *_: (0,) * n)


def _attn_kernel(seq, n_lat_steps, q_ref, kp_ref, kc_ref, kn_ref, vp_ref, vc_ref, vn_ref,
                 kx_ref, vx_ref, sink_ref, o_ref):
    i = pl.program_id(1)
    tq = q_ref.shape[0]
    n_ctx = kx_ref.shape[0]
    span = tq + 2 * WINDOW
    qpos = i * tq + lax.broadcasted_iota(I32, (tq, span), 0)
    kpos = i * tq - WINDOW + lax.broadcasted_iota(I32, (tq, span), 1)
    seq_eff = jnp.where(i < n_lat_steps, seq, 0)
    valid = (jnp.abs(qpos - kpos) <= WINDOW) & (kpos >= 0) & (kpos < seq_eff)
    bias = jnp.concatenate(
        [jnp.zeros((tq, n_ctx), F32), jnp.where(valid, 0.0, NEG_INF).astype(F32)], axis=1)
    n_keys = n_ctx + span
    for h in range(N_KV_HEADS):
        hs = slice(h * HEAD_DIM, (h + 1) * HEAD_DIM)
        kk = jnp.concatenate([kx_ref[:, hs], kp_ref[:, hs], kc_ref[:, hs], kn_ref[:, hs]], axis=0)
        vv = jnp.concatenate([vx_ref[:, hs], vp_ref[:, hs], vc_ref[:, hs], vn_ref[:, hs]], axis=0)
        heads = range(h * GROUP, (h + 1) * GROUP)
        qg = jnp.concatenate([q_ref[:, hq * HEAD_DIM:(hq + 1) * HEAD_DIM] for hq in heads], axis=0)
        sink = jnp.concatenate(
            [jnp.broadcast_to(sink_ref[0:1, hq:hq + 1], (1, tq, 1)) for hq in heads], axis=0)
        s = _dot_nt(qg, kk).reshape(GROUP, tq, n_keys) + bias[None]
        m = jnp.maximum(jnp.max(s, axis=-1, keepdims=True), sink)
        e = jnp.exp(s - m)
        den = jnp.sum(e, axis=-1, keepdims=True) + jnp.exp(sink - m)
        o = _dot(e.reshape(GROUP * tq, n_keys).astype(BF16), vv).reshape(GROUP, tq, HEAD_DIM) / den
        for g, hq in enumerate(heads):
            o_ref[:, hq * HEAD_DIM:(hq + 1) * HEAD_DIM] = o[g].astype(BF16)


def _attn_call(q, k, v, sink, batch, seq, n_ctx):
    rows, d = q.shape
    kvw = k.shape[1]
    tq = Q_TILE
    assert seq % tq == 0 and n_ctx % tq == 0 and tq % WINDOW == 0 and (batch * seq) % n_ctx == 0
    nl, nc = seq // tq, n_ctx // tq
    wb = seq // WINDOW
    r = tq // WINDOW

    def q_map(b, i):
        return (jnp.where(i < nl, b * nl + i, batch * nl + b * nc + (i - nl)), 0)

    def prev_map(b, i):
        return (b * wb + jnp.clip(i * r - 1, 0, wb - 1), 0)

    def cur_map(b, i):
        return (b * nl + jnp.minimum(i, nl - 1), 0)

    def next_map(b, i):
        return (b * wb + jnp.minimum((i + 1) * r, wb - 1), 0)

    def ctx_map(b, i):
        return ((batch * seq) // n_ctx + b, 0)

    win = lambda m: pl.BlockSpec((WINDOW, kvw), m)
    cur = pl.BlockSpec((tq, kvw), cur_map)
    ctxs = pl.BlockSpec((n_ctx, kvw), ctx_map)
    return pl.pallas_call(
        functools.partial(_attn_kernel, seq, nl),
        out_shape=jax.ShapeDtypeStruct((rows, d), BF16),
        grid=(batch, nl + nc),
        in_specs=[pl.BlockSpec((tq, d), q_map),
                  win(prev_map), cur, win(next_map), win(prev_map), cur, win(next_map),
                  ctxs, ctxs, _full(sink.shape)],
        out_specs=pl.BlockSpec((tq, d), q_map),
        compiler_params=_cparams("arbitrary", "arbitrary"),
        name="window_attn",
    )(q, k, k, k, v, v, v, k, v, sink)


def _shift_rows(x, k):
    return pltpu.roll(x, k % x.shape[0], 0)


def _lru_kernel(n_ctx_chunks, n_lat_chunks,
                pf_ref, cf_ref, nf_ref, pb_ref, cb_ref, nb_ref,
                cw_ref, cbias_ref, wa_ref, ba_ref, wx_ref, bx_ref, lam_ref,
                hf_ref, hb_ref, carry_ref, a_ref, b_ref, p_ref, h_ref):
    j = pl.program_id(1)
    tc, d = cf_ref.shape
    halo = pf_ref.shape[0]
    nblk = wa_ref.shape[1]
    bw = d // nblk
    is_ctx = j < n_ctx_chunks
    jf = jnp.where(is_ctx, j, j - n_ctx_chunks)
    nseq = jnp.where(is_ctx, n_ctx_chunks, n_lat_chunks)
    jb = nseq - 1 - jf

    @pl.when(j == 0)
    def _():
        carry_ref[...] = jnp.zeros(carry_ref.shape, F32)

    sub = lax.broadcasted_iota(I32, (SUBLANES, LANES), 0)
    run = tc // SUBLANES
    stride = a_ref.shape[1] // SUBLANES
    row_blk = lax.broadcasted_iota(I32, (tc, bw), 0)

    def coeffs(direction, prev_ref, c_ref, n_ref, jj):
        prev = prev_ref[...] * (jj > 0).astype(F32)
        nxt = n_ref[...] * (jj < nseq - 1).astype(F32)
        ext = jnp.concatenate([prev, c_ref[...], nxt], axis=0)
        xr = jnp.zeros((tc, d), F32) + cbias_ref[...]
        for k in range(cw_ref.shape[0]):
            xr = xr + cw_ref[k:k + 1, :] * _shift_rows(ext, 2 - k)[halo:halo + tc]
        for n in range(nblk):
            cs = slice(n * bw, (n + 1) * bw)
            xb = xr[:, cs].astype(BF16)
            r = jax.nn.sigmoid(_dot(xb, wa_ref[direction, n]) + ba_ref[direction:direction + 1, cs])
            gi = jax.nn.sigmoid(_dot(xb, wx_ref[direction, n]) + bx_ref[direction:direction + 1, cs])
            log_a = -LRU_C * r * jax.nn.softplus(-lam_ref[direction:direction + 1, cs])
            a = jnp.exp(log_a)
            mult = jnp.sqrt(1.0 - a * a)
            at_edge = is_ctx & (jj == (0 if direction == 0 else nseq - 1))
            reset_row = jnp.where(at_edge, 0 if direction == 0 else tc - 1, -1)
            mult = jnp.where(row_blk == reset_row, 1.0, mult)
            bb = mult * gi * xr[:, cs]
            for c in range(bw // LANES):
                for r in range(SUBLANES):
                    src = slice(r * run, (r + 1) * run)
                    dst = slice(r * stride, r * stride + run)
                    a_ref[n * (bw // LANES) + c, dst, :] = a[src, c * LANES:(c + 1) * LANES]
                    b_ref[n * (bw // LANES) + c, dst, :] = bb[src, c * LANES:(c + 1) * LANES]

    def scan(direction, out_ref):
        fwd = direction == 0
        for c in range(d // LANES):
            h = p = None
            for i in (range(run) if fwd else range(run - 1, -1, -1)):
                rows = pl.ds(i, SUBLANES, stride=stride)
                a_i, b_i = a_ref[c, rows, :], b_ref[c, rows, :]
                h, p = (b_i, a_i) if h is None else (a_i * h + b_i, a_i * p)
                h_ref[c, rows, :] = h
                p_ref[c, i * SUBLANES:(i + 1) * SUBLANES, :] = p
            for k in (1, 2, 4):
                keep = (sub >= k) if fwd else (sub < SUBLANES - k)
                p_sh, h_sh = _shift_rows(p, k if fwd else -k), _shift_rows(h, k if fwd else -k)
                h = jnp.where(keep, p * h_sh + h, h)
                p = jnp.where(keep, p * p_sh, p)
            lanes = slice(c * LANES, (c + 1) * LANES)
            carry = carry_ref[direction:direction + 1, lanes]
            end_state = p * carry + h
            incoming = jnp.where((sub >= 1) if fwd else (sub < SUBLANES - 1),
                                 _shift_rows(end_state, 1 if fwd else -1), carry)
            for i in range(run):
                rows = pl.ds(i, SUBLANES, stride=stride)
                h_ref[c, rows, :] = h_ref[c, rows, :] + p_ref[c, i * SUBLANES:(i + 1) * SUBLANES, :] * incoming
            for r in range(SUBLANES):
                out_ref[r * run:(r + 1) * run, lanes] = h_ref[c, r * stride:r * stride + run, :]
            carry_ref[direction:direction + 1, lanes] = (
                end_state[SUBLANES - 1:SUBLANES] if fwd else end_state[0:1])

    coeffs(0, pf_ref, cf_ref, nf_ref, jf)
    scan(0, hf_ref)
    coeffs(1, pb_ref, cb_ref, nb_ref, jb)
    scan(1, hb_ref)


def _lru_call(rec, conv_w, conv_b, wa, ba, wx, bx, lam, batch, seq, n_ctx):
    rows, d = rec.shape
    tc = SEQ_CHUNK
    halo = SUBLANES
    assert seq % tc == 0 and n_ctx % tc == 0
    ncc, nlc = n_ctx // tc, seq // tc
    hb_per_chunk = tc // halo
    last_halo = rows // halo - 1

    def chunk_f(b, j):
        return jnp.where(j < ncc, (batch * seq) // tc + b * ncc + j, b * nlc + (j - ncc))

    def chunk_b(b, j):
        return jnp.where(j < ncc, (batch * seq) // tc + b * ncc + (ncc - 1 - j),
                         b * nlc + (nlc - 1 - (j - ncc)))

    def cur(f):
        return pl.BlockSpec((tc, d), lambda b, j: (f(b, j), 0))

    def prev(f):
        return pl.BlockSpec((halo, d), lambda b, j: (jnp.maximum(f(b, j) * hb_per_chunk - 1, 0), 0))

    def nxt(f):
        return pl.BlockSpec((halo, d),
                            lambda b, j: (jnp.minimum((f(b, j) + 1) * hb_per_chunk, last_halo), 0))

    out = jax.ShapeDtypeStruct((rows, d), F32)
    return pl.pallas_call(
        functools.partial(_lru_kernel, ncc, nlc),
        out_shape=(out, out),
        grid=(batch, ncc + nlc),
        in_specs=[prev(chunk_f), cur(chunk_f), nxt(chunk_f), prev(chunk_b), cur(chunk_b), nxt(chunk_b),
                  _full(conv_w.shape), _full(conv_b.shape), _full(wa.shape), _full(ba.shape),
                  _full(wx.shape), _full(bx.shape), _full(lam.shape)],
        out_specs=(cur(chunk_f), cur(chunk_b)),
        scratch_shapes=[pltpu.VMEM((2, d), F32)]
        + [pltpu.VMEM((d // LANES, tc + SUBLANES * SCAN_RUN_PAD, LANES), F32)] * 2
        + [pltpu.VMEM((d // LANES, tc, LANES), F32),
           pltpu.VMEM((d // LANES, tc + SUBLANES * SCAN_RUN_PAD, LANES), F32)],
        compiler_params=_cparams("arbitrary", "arbitrary"),
        name="rglru_scan",
    )(rec, rec, rec, rec, rec, rec, conv_w, conv_b, wa, ba, wx, bx, lam)


def _dwconv_kernel(n_lat_chunks_total, lat_chunks, ctx_chunks,
                   p_ref, c_ref, n_ref, w_ref, b_ref, g_ref, beta_ref, o_ref):
    ci = pl.program_id(0)
    tc, d = c_ref.shape
    halo = p_ref.shape[0]
    ktaps = w_ref.shape[0]
    is_lat = ci < n_lat_chunks_total
    pos = jnp.where(is_lat, ci % lat_chunks, (ci - n_lat_chunks_total) % ctx_chunks)
    nseq = jnp.where(is_lat, lat_chunks, ctx_chunks)
    prev = p_ref[...] * (pos > 0).astype(F32)
    nxt = n_ref[...] * (pos < nseq - 1).astype(F32)
    ext = jnp.concatenate([prev, c_ref[...], nxt], axis=0)
    acc = jnp.zeros((tc, d), F32) + b_ref[...]
    rotated = [ext if r == 0 else _shift_rows(ext, r) for r in range(SUBLANES)]
    for k in range(ktaps):
        a, r = divmod(ktaps // 2 - k, SUBLANES)
        acc = acc + w_ref[k:k + 1, :] * rotated[r][halo - SUBLANES * a:halo - SUBLANES * a + tc]
    z = _layer_norm(acc, g_ref[...], beta_ref[...])
    o_ref[...] = (z * jax.nn.sigmoid(z)).astype(BF16)


def _dwconv_call(glu, w, b, g, beta, batch, seq, n_ctx):
    rows, d = glu.shape
    tc = SEQ_CHUNK
    halo = 2 * SUBLANES
    assert w.shape[0] // 2 <= halo and seq % tc == 0 and n_ctx % tc == 0
    per = tc // halo
    last_halo = rows // halo - 1
    return pl.pallas_call(
        functools.partial(_dwconv_kernel, batch * seq // tc, seq // tc, n_ctx // tc),
        out_shape=jax.ShapeDtypeStruct((rows, d), BF16),
        grid=(rows // tc,),
        in_specs=[pl.BlockSpec((halo, d), lambda i: (jnp.maximum(i * per - 1, 0), 0)),
                  pl.BlockSpec((tc, d), lambda i: (i, 0)),
                  pl.BlockSpec((halo, d), lambda i: (jnp.minimum((i + 1) * per, last_halo), 0)),
                  _full(w.shape), _full(b.shape), _full(g.shape), _full(beta.shape)],
        out_specs=pl.BlockSpec((tc, d), lambda i: (i, 0)),
        compiler_params=_cparams("arbitrary"),
        name="dwconv_ln_swish",
    )(glu, glu, glu, w, b, g, beta)


def _post_kernel(kind, has_bias, alpha, *refs):
    refs = list(refs)
    if kind == "lru":
        hf_ref, hb_ref, gate_ref = refs[:3]
        refs = refs[3:]
        core = ((hf_ref[...] + hb_ref[...]) * gate_ref[...]).astype(BF16)
    else:
        core = refs[0][...]
        refs = refs[1:]
    w_ref = refs[0]
    refs = refs[1:]
    y = _dot(core, w_ref[...])
    if has_bias:
        y = y + refs[0][...]
        refs = refs[1:]
    x_ref, mod_ref, lng_ref, lnb_ref, rt_ref, x1_ref, t_ref, aff_ref = refs
    m = mod_ref[0]
    x1 = _layer_norm(alpha * x_ref[...] + m[2:3] * y, lng_ref[...], lnb_ref[...])
    x1_ref[...] = x1
    t = (x1 * (1.0 + m[4:5]) + m[3:4]).astype(BF16)
    t_ref[...] = t
    logits = _dot_nt(rt_ref[...], t)
    e = jnp.exp(logits - jnp.max(logits, axis=0, keepdims=True))
    aff_ref[...] = e / jnp.sum(e, axis=0, keepdims=True)


def _post_call(kind, alpha, cores, w, bias, x_all, mod, ln_g, ln_b, router_t, n_rows, tiles_per_sample,
               n_samples):
    d = x_all.shape[1]
    tm = ROW_TILE
    n_e = router_t.shape[0]
    x_spec, mod_spec = _row_specs(tm, d, tiles_per_sample, n_samples)
    in_specs = [x_spec] * len(cores) + [_full(w.shape)]
    args = list(cores) + [w]
    if bias is not None:
        in_specs.append(_full(bias.shape))
        args.append(bias)
    in_specs += [x_spec, mod_spec, _full(ln_g.shape), _full(ln_b.shape), _full(router_t.shape)]
    args += [x_all, mod, ln_g, ln_b, router_t]
    return pl.pallas_call(
        functools.partial(_post_kernel, kind, bias is not None, alpha),
        out_shape=(jax.ShapeDtypeStruct((n_rows, d), F32),
                   jax.ShapeDtypeStruct((n_rows, d), BF16),
                   jax.ShapeDtypeStruct((n_e, n_rows), F32)),
        grid=(n_rows // tm,),
        in_specs=in_specs,
        out_specs=(x_spec, x_spec, pl.BlockSpec((n_e, tm), lambda i: (0, i))),
        compiler_params=_cparams("arbitrary"),
        name="mixer_out_ln_router",
    )(*args)


def _route_kernel(cap, slot_tile, aff_ref, pos_ref, excl_ref, lo_ref, hi_ref):
    n_e, seq = aff_ref.shape
    tb = min(TOKEN_BLOCK, seq)
    nblk = seq // tb
    bits = pltpu.bitcast(aff_ref[...], I32)

    def search(_, lohi):
        lo, hi = lohi
        mid = lo + jnp.right_shift(hi - lo, 1)
        cnt = jnp.sum((bits >= mid).astype(F32), axis=1, keepdims=True)
        ge = cnt >= cap
        return jnp.where(ge, mid, lo), jnp.where(ge, hi, mid)

    lo0 = jnp.zeros((n_e, 1), I32)
    hi0 = jnp.full((n_e, 1), 0x7F800000, I32)
    thr, _ = lax.fori_loop(0, 31, search, (lo0, hi0))
    n_gt = jnp.sum((bits > thr).astype(F32), axis=1, keepdims=True)
    need = cap - n_gt

    upper = (lax.broadcasted_iota(I32, (tb, tb), 0) < lax.broadcasted_iota(I32, (tb, tb), 1)).astype(BF16)
    lane = lax.broadcasted_iota(I32, (n_e, LANES), 1)
    eq_off = jnp.zeros((n_e, 1), F32)
    pos_off = jnp.zeros((n_e, 1), F32)
    excl = jnp.full((n_e, LANES), float(cap), F32)
    incl = jnp.full((n_e, LANES), 1e9, F32)
    for blk in range(nblk):
        cs = slice(blk * tb, (blk + 1) * tb)
        bb = bits[:, cs]
        gt = bb > thr
        eq = bb == thr
        eqf = jnp.where(eq, 1.0, 0.0)
        rank = _dot(eqf.astype(BF16), upper) + eq_off
        sel = gt | (eq & (rank < need))
        self_f = jnp.where(sel, 1.0, 0.0)
        pin = _dot(self_f.astype(BF16), upper) + pos_off
        pos_ref[:, cs] = jnp.where(sel, pin, -1.0)
        excl = jnp.where(lane == blk, pos_off, excl)
        eq_off = eq_off + jnp.sum(eqf, axis=1, keepdims=True)
        pos_off = pos_off + jnp.sum(self_f, axis=1, keepdims=True)
        incl = jnp.where(lane == blk, pos_off, incl)
    excl_ref[0] = excl.astype(I32)
    lo = jnp.zeros((n_e, LANES), F32)
    hi = jnp.zeros((n_e, LANES), F32)
    for q in range(cap // slot_tile):
        first = float(q * slot_tile)
        last = float(q * slot_tile + slot_tile - 1)
        lo_q = jnp.sum(jnp.where(incl <= first, 1.0, 0.0), axis=1, keepdims=True)
        hi_q = jnp.sum(jnp.where(incl <= last, 1.0, 0.0), axis=1, keepdims=True)
        lo = jnp.where(lane == q, lo_q, lo)
        hi = jnp.where(lane == q, jnp.minimum(hi_q, float(nblk - 1)), hi)
    lo_ref[0] = jnp.broadcast_to(jnp.min(lo, axis=0, keepdims=True), lo.shape).astype(I32)
    hi_ref[0] = jnp.broadcast_to(jnp.max(hi, axis=0, keepdims=True), hi.shape).astype(I32)


def _route_call(aff_t, batch, seq, first_block, cap, slot_tile):
    n_e = aff_t.shape[0]
    assert seq // min(TOKEN_BLOCK, seq) < LANES and cap // slot_tile <= LANES
    meta = jax.ShapeDtypeStruct((batch, n_e, LANES), I32)
    meta_spec = pl.BlockSpec((1, n_e, LANES), lambda b: (b, 0, 0))
    return pl.pallas_call(
        functools.partial(_route_kernel, cap, slot_tile),
        out_shape=(jax.ShapeDtypeStruct((n_e, batch * seq), F32), meta, meta, meta),
        grid=(batch,),
        in_specs=[pl.BlockSpec((n_e, seq), lambda b: (0, first_block + b))],
        out_specs=(pl.BlockSpec((n_e, seq), lambda b: (0, b)), meta_spec, meta_spec, meta_spec),
        compiler_params=_cparams("arbitrary"),
        name="expert_choice_route",
    )(aff_t)


def _dispatch_kernel(lo_ref, hi_ref, t_ref, pos_ref, aff_ref, xg_ref, gate_ref, acc_ref):
    b = pl.program_id(0)
    q = pl.program_id(1)
    n_e, slot_tile, d = xg_ref.shape[1], xg_ref.shape[2], xg_ref.shape[3]
    tb = pos_ref.shape[3]
    step = b * pl.num_programs(1) + q
    slot = (lax.broadcasted_iota(I32, (slot_tile, tb), 0) + q * slot_tile).astype(F32)

    def gather_block(blk):
        hits = [pos_ref[e, blk] == slot for e in range(n_e)]
        onehot = jnp.concatenate([jnp.where(h, 1.0, 0.0).astype(BF16) for h in hits], axis=0)
        x_blk = t_ref[pl.ds(pl.multiple_of(blk * tb, tb), tb), :]
        gates = [jnp.sum(jnp.where(h, aff_ref[e, blk], 0.0), axis=1, keepdims=True)
                 for e, h in enumerate(hits)]
        return _dot(onehot, x_blk), jnp.concatenate(gates, axis=0)

    def body(blk, gacc):
        rows, gates = gather_block(blk)
        acc_ref[...] += rows
        return gacc + gates

    lo = lo_ref[step]
    acc_ref[...], gacc = gather_block(lo)
    gacc = lax.fori_loop(lo + 1, hi_ref[step] + 1, body, gacc)
    for e in range(n_e):
        rows = slice(e * slot_tile, (e + 1) * slot_tile)
        xg_ref[0, e] = acc_ref[rows, :].astype(BF16)
        gate_ref[0, e] = gacc[rows]


def _dispatch_call(t_all, pos, aff_t, lo, hi, batch, seq, first_seq_block, aff_first_block, cap, slot_tile):
    d = t_all.shape[1]
    n_e = pos.shape[0]
    tb = min(TOKEN_BLOCK, seq)
    nblk = seq // tb
    pos4 = pos.reshape(n_e, batch * nblk, 1, tb)
    aff4 = aff_t.reshape(n_e, aff_t.shape[1] // tb, 1, tb)
    grid_spec = pltpu.PrefetchScalarGridSpec(
        num_scalar_prefetch=2,
        grid=(batch, cap // slot_tile),
        in_specs=[
            pl.BlockSpec((seq, d), lambda b, q, *_: (first_seq_block + b, 0), pipeline_mode=pl.Buffered(1)),
            pl.BlockSpec((n_e, nblk, 1, tb), lambda b, q, *_: (0, b, 0, 0)),
            pl.BlockSpec((n_e, nblk, 1, tb), lambda b, q, *_: (0, aff_first_block + b, 0, 0)),
        ],
        out_specs=(pl.BlockSpec((1, n_e, slot_tile, d), lambda b, q, *_: (b, 0, q, 0)),
                   pl.BlockSpec((1, n_e, slot_tile, 1), lambda b, q, *_: (b, 0, q, 0))),
        scratch_shapes=[pltpu.VMEM((n_e * slot_tile, d), F32)],
    )
    return pl.pallas_call(
        _dispatch_kernel,
        out_shape=(jax.ShapeDtypeStruct((batch, n_e, cap, d), BF16),
                   jax.ShapeDtypeStruct((batch, n_e, cap, 1), F32)),
        grid_spec=grid_spec,
        compiler_params=_cparams("arbitrary", "arbitrary"),
        name="moe_dispatch",
    )(lo.reshape(-1), hi.reshape(-1), t_all, pos4, aff4)


def _ffn_kernel(n_streams, w1_ref, w3_ref, w2_ref, *refs):
    ins, outs, accs = refs[:2 * n_streams], refs[2 * n_streams:3 * n_streams], refs[3 * n_streams:]
    f = pl.program_id(2)
    w1 = w1_ref[0, 0].astype(BF16)
    w3 = w3_ref[0, 0].astype(BF16)
    w2 = w2_ref[0, 0].astype(BF16)
    for s in range(n_streams):
        xg_ref, gate_ref, y_ref, acc_ref = ins[2 * s], ins[2 * s + 1], outs[s], accs[s]
        cap = acc_ref.shape[0]
        rt = min(FFN_ROW_TILE, cap)

        @pl.when(f == 0)
        def _(acc_ref=acc_ref):
            acc_ref[...] = jnp.zeros(acc_ref.shape, F32)

        for r in range(cap // rt):
            rows = slice(r * rt, (r + 1) * rt)
            xg = xg_ref[0, 0, rows, :]
            h1 = _dot(xg, w1)
            h3 = _dot(xg, w3)
            hid = (h1 * jax.nn.sigmoid(h1) * h3).astype(BF16)
            acc_ref[rows, :] += _dot(hid, w2)

        @pl.when(f == pl.num_programs(2) - 1)
        def _(acc_ref=acc_ref, gate_ref=gate_ref, y_ref=y_ref):
            y_ref[0, 0] = (acc_ref[...] * gate_ref[0, 0]).astype(BF16)


def _ffn_call(layer, streams, w1, w3, w2):
    n_e, d, ff = w1.shape[1:]
    batch = streams[0][0].shape[0]
    tf = FF_TILE
    tok = lambda e, b, f: (b, e, 0, 0)
    in_specs = [pl.BlockSpec((1, 1, d, tf), lambda e, b, f: (layer, e, 0, f)),
                pl.BlockSpec((1, 1, d, tf), lambda e, b, f: (layer, e, 0, f)),
                pl.BlockSpec((1, 1, tf, d), lambda e, b, f: (layer, e, f, 0))]
    args, out_shape, out_specs, scratch = [w1, w3, w2], [], [], []
    for xg, gate in streams:
        cap = xg.shape[2]
        assert cap % min(FFN_ROW_TILE, cap) == 0
        in_specs += [pl.BlockSpec((1, 1, cap, d), tok), pl.BlockSpec((1, 1, cap, 1), tok)]
        args += [xg, gate]
        out_shape.append(jax.ShapeDtypeStruct(xg.shape, BF16))
        out_specs.append(pl.BlockSpec((1, 1, cap, d), tok))
        scratch.append(pltpu.VMEM((cap, d), F32))
    return pl.pallas_call(
        functools.partial(_ffn_kernel, len(streams)),
        out_shape=tuple(out_shape),
        grid=(n_e, batch, ff // tf),
        in_specs=in_specs,
        out_specs=tuple(out_specs),
        scratch_shapes=scratch,
        compiler_params=_cparams("arbitrary", "arbitrary", "arbitrary"),
        name="expert_ffn",
    )(*args)


def _combine_kernel(n_e, group, nblk, n_tiles, excl_stride, has_alias, alpha, excl_ref, *refs):
    if has_alias:
        refs = refs[1:]
    (x1_ref, pos_ref, mod_ref, lng_ref, lnb_ref, y_hbm, o_ref,
     buf_ref, sem_ref, base_ref, exp_ref, count_ref) = refs
    tile_idx = pl.program_id(0)
    tb, d = x1_ref.shape
    per_tile = COMBINE_TILE_ROWS // group
    max_groups = base_ref.shape[0] // 2

    def group_copy(bb, e, base, par, g):
        return pltpu.make_async_copy(
            y_hbm.at[bb, e, pl.ds(pl.multiple_of(base, group), group)],
            buf_ref.at[par, pl.ds(pl.multiple_of(g * group, group), group)], sem_ref.at[par])

    def issue_tile(tile, par):
        bb, ii = tile // nblk, tile % nblk
        n_groups = jnp.int32(0)
        for e in range(n_e):
            first = excl_ref[(bb * n_e + e) * LANES + ii * excl_stride]
            end = excl_ref[(bb * n_e + e) * LANES + (ii + 1) * excl_stride]
            start = (first // group) * group
            n_run = jnp.where(end > first, (end - start + group - 1) // group, 0)

            def issue(k, g, e=e, start=start):
                base = start + k * group
                base_ref[par * max_groups + g] = base
                exp_ref[par * max_groups + g] = e
                group_copy(bb, e, base, par, g).start()
                return g + 1

            n_groups = lax.fori_loop(0, n_run, issue, n_groups)
        count_ref[par] = n_groups

    @pl.when(tile_idx < n_tiles)
    def _():
        par = tile_idx % 2

        @pl.when(tile_idx == 0)
        def _():
            buf_ref[...] = jnp.zeros(buf_ref.shape, BF16)
            issue_tile(tile_idx, par)

        @pl.when(tile_idx + 1 < n_tiles)
        def _():
            issue_tile(tile_idx + 1, 1 - par)

        n_groups = count_ref[par]

        def wait_one(_, carry):
            group_copy(0, 0, 0, par, 0).wait()
            return carry

        lax.fori_loop(0, n_groups, wait_one, 0)
        row = lax.broadcasted_iota(I32, (group, tb), 0).astype(F32)

        def reduce_groups(first_group, n, f):
            parts = []
            for j in range(n):
                g = first_group + j
                live = g < n_groups
                gc = par * max_groups + jnp.minimum(g, n_groups - 1)
                base = jnp.where(live, base_ref[gc], -2 * group).astype(F32)
                hit = (pos_ref[exp_ref[gc]] - base) == row
                parts.append(jnp.where(hit, 1.0, 0.0).astype(BF16))
            onehot = jnp.concatenate(parts, axis=0)
            rows = buf_ref[par, pl.ds(pl.multiple_of(first_group * group, small * group), n * group), :]
            return f + _dot_tn(onehot, rows)

        small = MXU_DEPTH // group
        n_big = n_groups // per_tile
        f = lax.fori_loop(0, n_big, lambda kt, f: reduce_groups(kt * per_tile, per_tile, f),
                          jnp.zeros((tb, d), F32))
        n_small = (n_groups - n_big * per_tile + small - 1) // small
        f = lax.fori_loop(0, n_small, lambda kt, f: reduce_groups(n_big * per_tile + kt * small, small, f), f)
        m = mod_ref[0]
        o_ref[...] = _layer_norm(alpha * x1_ref[...] + m[5:6] * f, lng_ref[...], lnb_ref[...])

    @pl.when(tile_idx >= n_tiles)
    def _():
        o_ref[...] = x1_ref[...]


def _combine_call(alpha, x1, pos, excl, y, mod, ln_g, ln_b, batch, seq, x_first_block, mod_row, out_rows,
                  alias_buf, carry_first_block=0, carry_blocks=0):
    d = x1.shape[1]
    n_e, cap = y.shape[1], y.shape[2]
    tb = min(COMBINE_TOKENS, seq)
    nblk = seq // tb
    excl_stride = tb // min(TOKEN_BLOCK, seq)
    group = COMBINE_GROUP
    assert cap % group == 0 and COMBINE_TILE_ROWS % group == 0 and group % BF16_ROWS == 0
    assert seq % tb == 0 and tb % min(TOKEN_BLOCK, seq) == 0
    max_groups = n_e * min(tb // group + 1, cap // group)
    buf_rows = -(-max_groups * group // COMBINE_TILE_ROWS) * COMBINE_TILE_ROWS
    has_alias = alias_buf is not None

    n_tiles = batch * nblk

    def row_map(s, *_):
        return (jnp.where(s < n_tiles, x_first_block + s, carry_first_block + (s - n_tiles)), 0)

    def mod_map(s, *_):
        return (jnp.minimum(s // nblk, batch - 1) if mod_row is None else mod_row, 0, 0)

    in_specs = [pl.BlockSpec((tb, d), row_map),
                pl.BlockSpec((n_e, 1, tb), lambda s, *_: (0, 0, jnp.minimum(s, n_tiles - 1))),
                pl.BlockSpec((1, 6, d), mod_map),
                _full(ln_g.shape), _full(ln_b.shape),
                pl.BlockSpec(memory_space=pl.ANY)]
    args = [x1, pos.reshape(n_e, 1, -1), mod, ln_g, ln_b, y]
    aliases = {}
    if has_alias:
        in_specs = [pl.BlockSpec(memory_space=pl.ANY)] + in_specs
        args = [alias_buf] + args
        aliases = {1: 0}
    grid_spec = pltpu.PrefetchScalarGridSpec(
        num_scalar_prefetch=1,
        grid=(n_tiles + carry_blocks,),
        in_specs=in_specs,
        out_specs=pl.BlockSpec((tb, d), row_map),
        scratch_shapes=[pltpu.VMEM((2, buf_rows, d), BF16), pltpu.SemaphoreType.DMA((2,)),
                        pltpu.SMEM((2 * max_groups,), I32), pltpu.SMEM((2 * max_groups,), I32),
                        pltpu.SMEM((2,), I32)],
    )
    return pl.pallas_call(
        functools.partial(_combine_kernel, n_e, group, nblk, n_tiles, excl_stride, has_alias, alpha),
        out_shape=jax.ShapeDtypeStruct((out_rows, d), F32),
        grid_spec=grid_spec,
        input_output_aliases=aliases,
        compiler_params=_cparams("arbitrary"),
        name="moe_combine_ln",
    )(excl.reshape(-1), *args)


def _rope_tables(seq, pad_rows):
    rows = seq // GRID_W
    row = jnp.broadcast_to(jnp.arange(rows, dtype=F32)[:, None], (rows, GRID_W)).reshape(-1)
    col = jnp.broadcast_to(jnp.arange(GRID_W, dtype=F32)[None, :], (rows, GRID_W)).reshape(-1)
    n_freq = HEAD_DIM // 4
    inv = ROPE_BASE ** (-jnp.arange(n_freq, dtype=F32) / n_freq)
    ang = jnp.concatenate([row[:, None] * inv, col[:, None] * inv], axis=-1)
    cos, sin = jnp.cos(ang), jnp.sin(ang)
    reps = LANES // HEAD_DIM
    cos_t = jnp.tile(jnp.concatenate([cos, cos], axis=-1), (1, reps))
    sin_t = jnp.tile(jnp.concatenate([-sin, sin], axis=-1), (1, reps))
    cos_t = jnp.concatenate([cos_t, jnp.ones((pad_rows, LANES), F32)], axis=0)
    sin_t = jnp.concatenate([sin_t, jnp.zeros((pad_rows, LANES), F32)], axis=0)
    return cos_t, sin_t


def _moe(alpha, layer, x1, t, aff_t, mod, ln_g, ln_b, w1, w3, w2, batch, seq, n_ctx, with_ctx, last):
    n_e = aff_t.shape[0]
    n_lat = batch * seq
    cap = max(1, EC_CAPACITY_FACTOR * seq // n_e)
    st = min(SLOT_TILE, cap)
    pos, excl, lo, hi = _route_call(aff_t, batch, seq, 0, cap, st)
    streams = [_dispatch_call(t, pos, aff_t, lo[:, 0, :cap // st], hi[:, 0, :cap // st],
                              batch, seq, 0, 0, cap, st)]
    if with_ctx:
        capc = max(1, EC_CAPACITY_FACTOR * n_ctx // n_e)
        stc = min(SLOT_TILE, capc)
        tbc = min(TOKEN_BLOCK, n_ctx)
        posc, exclc, loc, hic = _route_call(aff_t, batch, n_ctx, n_lat // n_ctx, capc, stc)
        streams.append(_dispatch_call(t, posc, aff_t, loc[:, 0, :capc // stc], hic[:, 0, :capc // stc],
                                      batch, n_ctx, n_lat // n_ctx, n_lat // tbc, capc, stc))
    ys = _ffn_call(layer, streams, w1, w3, w2)
    out_rows = n_lat if last else x1.shape[0]
    tb = min(COMBINE_TOKENS, seq)
    assert last or (batch * n_ctx) % tb == 0
    x2 = _combine_call(alpha, x1, pos, excl, ys[0], mod, ln_g, ln_b, batch, seq, 0, None, out_rows, None,
                       carry_first_block=n_lat // tb, carry_blocks=0 if last else (batch * n_ctx) // tb)
    if with_ctx:
        x2 = _combine_call(alpha, x1, posc, exclc, ys[1], mod, ln_g, ln_b, batch, n_ctx,
                           n_lat // min(COMBINE_TOKENS, n_ctx), batch, x1.shape[0], x2)
    return x2


def kernel(x, c, ctx, c_ctx, ada_w, ada_b, ln1_g, ln1_b, ln2_g, ln2_b, attn_w_qkv, attn_sink, attn_w_o, lru_w_in, lru_conv_w, lru_conv_b, lru_gate_a_w, lru_gate_a_b, lru_gate_x_w, lru_gate_x_b, lru_lambda, lru_w_out, conv_w_in, conv_b_in, conv_dw_w, conv_dw_b, conv_ln_g, conv_ln_b, conv_w_out, conv_b_out, moe_router, moe_w1, moe_w3, moe_w2):
    batch, seq, d = x.shape
    n_ctx = ctx.shape[1]
    depth = ada_w.shape[0]
    n_lat = batch * seq
    rows = n_lat + batch * n_ctx
    tm = ROW_TILE
    assert seq % tm == 0 and (batch * n_ctx) % tm == 0
    tiles_per_sample = seq // tm
    alpha = float((2 * depth) ** 0.25)

    x_all = jnp.concatenate([x.reshape(n_lat, d), ctx.reshape(batch * n_ctx, d)], axis=0)
    cond = jnp.zeros((SUBLANES, d), F32).at[:batch].set(c).at[batch].set(c_ctx)
    mods = _ada_call(cond, ada_w, ada_b)[:, :batch + 1].reshape(depth, batch + 1, 6, d)
    cos_t, sin_t = _rope_tables(seq, tm)
    x_spec, mod_spec = _row_specs(tm, d, tiles_per_sample, batch)
    vec = lambda a: a.reshape(1, -1)

    for i in range(depth):
        last = i == depth - 1
        kind, j = i % 3, i // 3
        mod = mods[i]
        n_rows = rows
        if kind == 0:
            w_qkv = attn_w_qkv[j].astype(BF16)
            kvw = N_KV_HEADS * HEAD_DIM
            rope_spec = pl.BlockSpec(
                (tm, LANES), lambda t: (jnp.where(t < batch * tiles_per_sample, t % tiles_per_sample,
                                                  tiles_per_sample), 0))
            q, k, v = pl.pallas_call(
                _attn_in_kernel,
                out_shape=(jax.ShapeDtypeStruct((rows, d), BF16),
                           jax.ShapeDtypeStruct((rows, kvw), BF16),
                           jax.ShapeDtypeStruct((rows, kvw), BF16)),
                grid=(rows // tm,),
                in_specs=[x_spec, mod_spec, _full(w_qkv.shape), rope_spec, rope_spec],
                out_specs=(x_spec, pl.BlockSpec((tm, kvw), lambda t: (t, 0)),
                           pl.BlockSpec((tm, kvw), lambda t: (t, 0))),
                compiler_params=_cparams("arbitrary"),
                name="attn_in_proj",
            )(x_all, mod, w_qkv, cos_t, sin_t)
            o = _attn_call(q, k, v, vec(attn_sink[j]), batch, seq, n_ctx)
            cores, w_out, b_out, pkind = [o], attn_w_o[j].astype(BF16), None, "direct"
        elif kind == 1:
            w_in = lru_w_in[j].astype(BF16)
            gate, rec = pl.pallas_call(
                _lru_in_kernel,
                out_shape=(jax.ShapeDtypeStruct((rows, d), F32), jax.ShapeDtypeStruct((rows, d), F32)),
                grid=(rows // tm,),
                in_specs=[x_spec, mod_spec, _full(w_in.shape)],
                out_specs=(x_spec, x_spec),
                compiler_params=_cparams("arbitrary"),
                name="lru_in_proj",
            )(x_all, mod, w_in)
            hf, hb = _lru_call(rec, lru_conv_w[j], vec(lru_conv_b[j]),
                               lru_gate_a_w[j].astype(BF16), lru_gate_a_b[j],
                               lru_gate_x_w[j].astype(BF16), lru_gate_x_b[j], lru_lambda[j],
                               batch, seq, n_ctx)
            cores, w_out, b_out, pkind = [hf, hb, gate], lru_w_out[j].astype(BF16), None, "lru"
        else:
            w_in = conv_w_in[j].astype(BF16)
            glu = pl.pallas_call(
                _conv_in_kernel,
                out_shape=jax.ShapeDtypeStruct((rows, d), F32),
                grid=(rows // tm,),
                in_specs=[x_spec, mod_spec, _full(w_in.shape), _full((1, 2 * d))],
                out_specs=x_spec,
                compiler_params=_cparams("arbitrary"),
                name="conv_in_proj",
            )(x_all, mod, w_in, vec(conv_b_in[j]))
            core = _dwconv_call(glu, conv_dw_w[j], vec(conv_dw_b[j]), vec(conv_ln_g[j]), vec(conv_ln_b[j]),
                                batch, seq, n_ctx)
            cores, w_out, b_out, pkind = [core], conv_w_out[j].astype(BF16), vec(conv_b_out[j]), "direct"
        post_rows = n_lat if last else rows
        x1, t, aff_t = _post_call(pkind, alpha, cores, w_out, b_out, x_all, mod, vec(ln1_g[i]), vec(ln1_b[i]),
                                  moe_router[i].T.astype(BF16), post_rows, tiles_per_sample, batch)
        x_all = _moe(alpha, i, x1, t, aff_t, mod, vec(ln2_g[i]), vec(ln2_b[i]), moe_w1, moe_w3, moe_w2,
                     batch, seq, n_ctx, not last, last)
    return x_all.reshape(batch, seq, d)
```

```python
import functools

import jax
import jax.numpy as jnp
from jax import lax
from jax.experimental import pallas as pl
from jax.experimental.pallas import tpu as pltpu

F32 = jnp.float32
BF16 = jnp.bfloat16
I32 = jnp.int32

HEAD_DIM = 64
N_KV_HEADS = 4
GROUP = 4
WINDOW = 128
GRID_W = 64
ROPE_BASE = 10000.0
N_EXPERTS = 16
EC_CAPACITY_FACTOR = 2
LRU_C = 8.0
LN_EPS = 1e-5
NEG_INF = -1e30

ROW_TILE = 512
Q_TILE = 128
SEQ_CHUNK = 256
SCAN_RUN_PAD = 8
TOKEN_BLOCK = 256
COMBINE_TOKENS = 512
SLOT_TILE = 64
FF_TILE = 512
FFN_ROW_TILE = 512
COMBINE_GROUP = 32
MXU_DEPTH = 256
COMBINE_TILE_ROWS = 1024
LANES = 128
SUBLANES = 8
BF16_ROWS = 16
VMEM_LIMIT = 56 * 1024 * 1024


def _cparams(*sem):
    return pltpu.CompilerParams(dimension_semantics=sem, vmem_limit_bytes=VMEM_LIMIT)


def _layer_norm(y, g, b):
    mu = jnp.mean(y, axis=-1, keepdims=True)
    d = y - mu
    var = jnp.mean(d * d, axis=-1, keepdims=True)
    return d * lax.rsqrt(var + LN_EPS) * g + b


def _dot(a, b):
    return jnp.dot(a, b, preferred_element_type=F32)


def _dot_nt(a, b):
    return lax.dot_general(a, b, (((1,), (1,)), ((), ())), preferred_element_type=F32)


def _dot_tn(a, b):
    return lax.dot_general(a, b, (((0,), (0,)), ((), ())), preferred_element_type=F32)


def _ada_kernel(c_ref, w_ref, b_ref, o_ref):
    cnd = c_ref[...]
    s = cnd * jax.nn.sigmoid(cnd)
    o_ref[0] = _dot(s.astype(BF16), w_ref[0].astype(BF16)) + b_ref[0]


def _ada_call(cond, ada_w, ada_b):
    depth, d, n6 = ada_w.shape
    rows = cond.shape[0]
    tn = 1536
    return pl.pallas_call(
        _ada_kernel,
        out_shape=jax.ShapeDtypeStruct((depth, rows, n6), F32),
        grid=(depth, n6 // tn),
        in_specs=[
            pl.BlockSpec((rows, d), lambda l, n: (0, 0)),
            pl.BlockSpec((1, d, tn), lambda l, n: (l, 0, n)),
            pl.BlockSpec((1, 1, tn), lambda l, n: (l, 0, n)),
        ],
        out_specs=pl.BlockSpec((1, rows, tn), lambda l, n: (l, 0, n)),
        compiler_params=_cparams("arbitrary", "arbitrary"),
        name="ada_mod",
    )(cond, ada_w, ada_b.reshape(depth, 1, n6))


def _modulate1(x_ref, mod_ref):
    m = mod_ref[0]
    return (x_ref[...] * (1.0 + m[1:2]) + m[0:1]).astype(BF16)


def _attn_in_kernel(x_ref, mod_ref, w_ref, cos_ref, sin_ref, q_ref, k_ref, v_ref):
    d = x_ref.shape[1]
    z = _dot(_modulate1(x_ref, mod_ref), w_ref[...])
    cos = cos_ref[...]
    sin = sin_ref[...]
    lane = lax.broadcasted_iota(I32, cos.shape, 1)
    first_half = (lane % HEAD_DIM) < (HEAD_DIM // 2)

    def rope(t):
        swapped = jnp.where(first_half, pltpu.roll(t, LANES - HEAD_DIM // 2, 1),
                            pltpu.roll(t, HEAD_DIM // 2, 1))
        return t * cos + swapped * sin

    scale = HEAD_DIM ** -0.5
    for cidx in range(d // LANES):
        sl = slice(cidx * LANES, (cidx + 1) * LANES)
        q_ref[:, sl] = (rope(z[:, sl]) * scale).astype(BF16)
    kvw = k_ref.shape[1]
    for cidx in range(kvw // LANES):
        sl = slice(cidx * LANES, (cidx + 1) * LANES)
        k_ref[:, sl] = rope(z[:, d + cidx * LANES:d + (cidx + 1) * LANES]).astype(BF16)
    v_ref[...] = z[:, d + kvw:].astype(BF16)


def _lru_in_kernel(x_ref, mod_ref, w_ref, g_ref, rec_ref):
    d = x_ref.shape[1]
    z = _dot(_modulate1(x_ref, mod_ref), w_ref[...])
    g_ref[...] = jax.nn.gelu(z[:, :d]).astype(g_ref.dtype)
    rec_ref[...] = z[:, d:]


def _conv_in_kernel(x_ref, mod_ref, w_ref, b_ref, o_ref):
    d = x_ref.shape[1]
    z = _dot(_modulate1(x_ref, mod_ref), w_ref[...]) + b_ref[...]
    o_ref[...] = z[:, :d] * jax.nn.sigmoid(z[:, d:])


def _row_specs(tm, d, tiles_per_sample, n_samples):
    x_spec = pl.BlockSpec((tm, d), lambda i: (i, 0))
    mod_spec = pl.BlockSpec((1, 6, d), lambda i: (jnp.minimum(i // tiles_per_sample, n_samples), 0, 0))
    return x_spec, mod_spec


def _full(shape):
    n = len(shape)
    return pl.BlockSpec(shape, lambda *_: (0,) * n)


def _attn_kernel(seq, n_lat_steps, q_ref, kp_ref, kc_ref, kn_ref, vp_ref, vc_ref, vn_ref,
                 kx_ref, vx_ref, sink_ref, o_ref):
    i = pl.program_id(1)
    tq = q_ref.shape[0]
    n_ctx = kx_ref.shape[0]
    span = tq + 2 * WINDOW
    qpos = i * tq + lax.broadcasted_iota(I32, (tq, span), 0)
    kpos = i * tq - WINDOW + lax.broadcasted_iota(I32, (tq, span), 1)
    seq_eff = jnp.where(i < n_lat_steps, seq, 0)
    valid = (jnp.abs(qpos - kpos) <= WINDOW) & (kpos >= 0) & (kpos < seq_eff)
    bias = jnp.concatenate(
        [jnp.zeros((tq, n_ctx), F32), jnp.where(valid, 0.0, NEG_INF).astype(F32)], axis=1)
    n_keys = n_ctx + span
    for h in range(N_KV_HEADS):
        hs = slice(h * HEAD_DIM, (h + 1) * HEAD_DIM)
        kk = jnp.concatenate([kx_ref[:, hs], kp_ref[:, hs], kc_ref[:, hs], kn_ref[:, hs]], axis=0)
        vv = jnp.concatenate([vx_ref[:, hs], vp_ref[:, hs], vc_ref[:, hs], vn_ref[:, hs]], axis=0)
        heads = range(h * GROUP, (h + 1) * GROUP)
        qg = jnp.concatenate([q_ref[:, hq * HEAD_DIM:(hq + 1) * HEAD_DIM] for hq in heads], axis=0)
        sink = jnp.concatenate(
            [jnp.broadcast_to(sink_ref[0:1, hq:hq + 1], (1, tq, 1)) for hq in heads], axis=0)
        s = _dot_nt(qg, kk).reshape(GROUP, tq, n_keys) + bias[None]
        m = jnp.maximum(jnp.max(s, axis=-1, keepdims=True), sink)
        e = jnp.exp(s - m)
        den = jnp.sum(e, axis=-1, keepdims=True) + jnp.exp(sink - m)
        o = _dot(e.reshape(GROUP * tq, n_keys).astype(BF16), vv).reshape(GROUP, tq, HEAD_DIM) / den
        for g, hq in enumerate(heads):
            o_ref[:, hq * HEAD_DIM:(hq + 1) * HEAD_DIM] = o[g].astype(BF16)


def _attn_call(q, k, v, sink, batch, seq, n_ctx):
    rows, d = q.shape
    kvw = k.shape[1]
    tq = Q_TILE
    assert seq % tq == 0 and n_ctx % tq == 0 and tq % WINDOW == 0 and (batch * seq) % n_ctx == 0
    nl, nc = seq // tq, n_ctx // tq
    wb = seq // WINDOW
    r = tq // WINDOW

    def q_map(b, i):
        return (jnp.where(i < nl, b * nl + i, batch * nl + b * nc + (i - nl)), 0)

    def prev_map(b, i):
        return (b * wb + jnp.clip(i * r - 1, 0, wb - 1), 0)

    def cur_map(b, i):
        return (b * nl + jnp.minimum(i, nl - 1), 0)

    def next_map(b, i):
        return (b * wb + jnp.minimum((i + 1) * r, wb - 1), 0)

    def ctx_map(b, i):
        return ((batch * seq) // n_ctx + b, 0)

    win = lambda m: pl.BlockSpec((WINDOW, kvw), m)
    cur = pl.BlockSpec((tq, kvw), cur_map)
    ctxs = pl.BlockSpec((n_ctx, kvw), ctx_map)
    return pl.pallas_call(
        functools.partial(_attn_kernel, seq, nl),
        out_shape=jax.ShapeDtypeStruct((rows, d), BF16),
        grid=(batch, nl + nc),
        in_specs=[pl.BlockSpec((tq, d), q_map),
                  win(prev_map), cur, win(next_map), win(prev_map), cur, win(next_map),
                  ctxs, ctxs, _full(sink.shape)],
        out_specs=pl.BlockSpec((tq, d), q_map),
        compiler_params=_cparams("arbitrary", "arbitrary"),
        name="window_attn",
    )(q, k, k, k, v, v, v, k, v, sink)


def _shift_rows(x, k):
    return pltpu.roll(x, k % x.shape[0], 0)


def _lru_kernel(n_ctx_chunks, n_lat_chunks,
                pf_ref, cf_ref, nf_ref, pb_ref, cb_ref, nb_ref,
                cw_ref, cbias_ref, wa_ref, ba_ref, wx_ref, bx_ref, lam_ref,
                hf_ref, hb_ref, carry_ref, a_ref, b_ref, p_ref, h_ref):
    j = pl.program_id(1)
    tc, d = cf_ref.shape
    halo = pf_ref.shape[0]
    nblk = wa_ref.shape[1]
    bw = d // nblk
    is_ctx = j < n_ctx_chunks
    jf = jnp.where(is_ctx, j, j - n_ctx_chunks)
    nseq = jnp.where(is_ctx, n_ctx_chunks, n_lat_chunks)
    jb = nseq - 1 - jf

    @pl.when(j == 0)
    def _():
        carry_ref[...] = jnp.zeros(carry_ref.shape, F32)

    sub = lax.broadcasted_iota(I32, (SUBLANES, LANES), 0)
    run = tc // SUBLANES
    stride = a_ref.shape[1] // SUBLANES
    row_blk = lax.broadcasted_iota(I32, (tc, bw), 0)

    def coeffs(direction, prev_ref, c_ref, n_ref, jj):
        prev = prev_ref[...] * (jj > 0).astype(F32)
        nxt = n_ref[...] * (jj < nseq - 1).astype(F32)
        ext = jnp.concatenate([prev, c_ref[...], nxt], axis=0)
        xr = jnp.zeros((tc, d), F32) + cbias_ref[...]
        for k in range(cw_ref.shape[0]):
            xr = xr + cw_ref[k:k + 1, :] * _shift_rows(ext, 2 - k)[halo:halo + tc]
        for n in range(nblk):
            cs = slice(n * bw, (n + 1) * bw)
            xb = xr[:, cs].astype(BF16)
            r = jax.nn.sigmoid(_dot(xb, wa_ref[direction, n]) + ba_ref[direction:direction + 1, cs])
            gi = jax.nn.sigmoid(_dot(xb, wx_ref[direction, n]) + bx_ref[direction:direction + 1, cs])
            log_a = -LRU_C * r * jax.nn.softplus(-lam_ref[direction:direction + 1, cs])
            a = jnp.exp(log_a)
            mult = jnp.sqrt(1.0 - a * a)
            at_edge = is_ctx & (jj == (0 if direction == 0 else nseq - 1))
            reset_row = jnp.where(at_edge, 0 if direction == 0 else tc - 1, -1)
            mult = jnp.where(row_blk == reset_row, 1.0, mult)
            bb = mult * gi * xr[:, cs]
            for c in range(bw // LANES):
                for r in range(SUBLANES):
                    src = slice(r * run, (r + 1) * run)
                    dst = slice(r * stride, r * stride + run)
                    a_ref[n * (bw // LANES) + c, dst, :] = a[src, c * LANES:(c + 1) * LANES]
                    b_ref[n * (bw // LANES) + c, dst, :] = bb[src, c * LANES:(c + 1) * LANES]

    def scan(direction, out_ref):
        fwd = direction == 0
        for c in range(d // LANES):
            h = p = None
            for i in (range(run) if fwd else range(run - 1, -1, -1)):
                rows = pl.ds(i, SUBLANES, stride=stride)
                a_i, b_i = a_ref[c, rows, :], b_ref[c, rows, :]
                h, p = (b_i, a_i) if h is None else (a_i * h + b_i, a_i * p)
                h_ref[c, rows, :] = h
                p_ref[c, i * SUBLANES:(i + 1) * SUBLANES, :] = p
            for k in (1, 2, 4):
                keep = (sub >= k) if fwd else (sub < SUBLANES - k)
                p_sh, h_sh = _shift_rows(p, k if fwd else -k), _shift_rows(h, k if fwd else -k)
                h = jnp.where(keep, p * h_sh + h, h)
                p = jnp.where(keep, p * p_sh, p)
            lanes = slice(c * LANES, (c + 1) * LANES)
            carry = carry_ref[direction:direction + 1, lanes]
            end_state = p * carry + h
            incoming = jnp.where((sub >= 1) if fwd else (sub < SUBLANES - 1),
                                 _shift_rows(end_state, 1 if fwd else -1), carry)
            for i in range(run):
                rows = pl.ds(i, SUBLANES, stride=stride)
                h_ref[c, rows, :] = h_ref[c, rows, :] + p_ref[c, i * SUBLANES:(i + 1) * SUBLANES, :] * incoming
            for r in range(SUBLANES):
                out_ref[r * run:(r + 1) * run, lanes] = (
                    h_ref[c, r * stride:r * stride + run, :].astype(out_ref.dtype))
            carry_ref[direction:direction + 1, lanes] = (
                end_state[SUBLANES - 1:SUBLANES] if fwd else end_state[0:1])

    coeffs(0, pf_ref, cf_ref, nf_ref, jf)
    scan(0, hf_ref)
    coeffs(1, pb_ref, cb_ref, nb_ref, jb)
    scan(1, hb_ref)


def _lru_call(rec, conv_w, conv_b, wa, ba, wx, bx, lam, batch, seq, n_ctx):
    rows, d = rec.shape
    tc = SEQ_CHUNK
    halo = SUBLANES
    assert seq % tc == 0 and n_ctx % tc == 0
    ncc, nlc = n_ctx // tc, seq // tc
    hb_per_chunk = tc // halo
    last_halo = rows // halo - 1

    def chunk_f(b, j):
        return jnp.where(j < ncc, (batch * seq) // tc + b * ncc + j, b * nlc + (j - ncc))

    def chunk_b(b, j):
        return jnp.where(j < ncc, (batch * seq) // tc + b * ncc + (ncc - 1 - j),
                         b * nlc + (nlc - 1 - (j - ncc)))

    def cur(f):
        return pl.BlockSpec((tc, d), lambda b, j: (f(b, j), 0))

    def prev(f):
        return pl.BlockSpec((halo, d), lambda b, j: (jnp.maximum(f(b, j) * hb_per_chunk - 1, 0), 0))

    def nxt(f):
        return pl.BlockSpec((halo, d),
                            lambda b, j: (jnp.minimum((f(b, j) + 1) * hb_per_chunk, last_halo), 0))

    out = jax.ShapeDtypeStruct((rows, d), BF16)
    return pl.pallas_call(
        functools.partial(_lru_kernel, ncc, nlc),
        out_shape=(out, out),
        grid=(batch, ncc + nlc),
        in_specs=[prev(chunk_f), cur(chunk_f), nxt(chunk_f), prev(chunk_b), cur(chunk_b), nxt(chunk_b),
                  _full(conv_w.shape), _full(conv_b.shape), _full(wa.shape), _full(ba.shape),
                  _full(wx.shape), _full(bx.shape), _full(lam.shape)],
        out_specs=(cur(chunk_f), cur(chunk_b)),
        scratch_shapes=[pltpu.VMEM((2, d), F32)]
        + [pltpu.VMEM((d // LANES, tc + SUBLANES * SCAN_RUN_PAD, LANES), F32)] * 2
        + [pltpu.VMEM((d // LANES, tc, LANES), F32),
           pltpu.VMEM((d // LANES, tc + SUBLANES * SCAN_RUN_PAD, LANES), F32)],
        compiler_params=_cparams("arbitrary", "arbitrary"),
        name="rglru_scan",
    )(rec, rec, rec, rec, rec, rec, conv_w, conv_b, wa, ba, wx, bx, lam)


def _dwconv_kernel(n_lat_chunks_total, lat_chunks, ctx_chunks,
                   p_ref, c_ref, n_ref, w_ref, b_ref, g_ref, beta_ref, o_ref):
    ci = pl.program_id(0)
    tc, d = c_ref.shape
    halo = p_ref.shape[0]
    ktaps = w_ref.shape[0]
    is_lat = ci < n_lat_chunks_total
    pos = jnp.where(is_lat, ci % lat_chunks, (ci - n_lat_chunks_total) % ctx_chunks)
    nseq = jnp.where(is_lat, lat_chunks, ctx_chunks)
    prev = p_ref[...] * (pos > 0).astype(F32)
    nxt = n_ref[...] * (pos < nseq - 1).astype(F32)
    ext = jnp.concatenate([prev, c_ref[...], nxt], axis=0)
    acc = jnp.zeros((tc, d), F32) + b_ref[...]
    rotated = [ext if r == 0 else _shift_rows(ext, r) for r in range(SUBLANES)]
    for k in range(ktaps):
        a, r = divmod(ktaps // 2 - k, SUBLANES)
        acc = acc + w_ref[k:k + 1, :] * rotated[r][halo - SUBLANES * a:halo - SUBLANES * a + tc]
    z = _layer_norm(acc, g_ref[...], beta_ref[...])
    o_ref[...] = (z * jax.nn.sigmoid(z)).astype(BF16)


def _dwconv_call(glu, w, b, g, beta, batch, seq, n_ctx):
    rows, d = glu.shape
    tc = SEQ_CHUNK
    halo = 2 * SUBLANES
    assert w.shape[0] // 2 <= halo and seq % tc == 0 and n_ctx % tc == 0
    per = tc // halo
    last_halo = rows // halo - 1
    return pl.pallas_call(
        functools.partial(_dwconv_kernel, batch * seq // tc, seq // tc, n_ctx // tc),
        out_shape=jax.ShapeDtypeStruct((rows, d), BF16),
        grid=(rows // tc,),
        in_specs=[pl.BlockSpec((halo, d), lambda i: (jnp.maximum(i * per - 1, 0), 0)),
                  pl.BlockSpec((tc, d), lambda i: (i, 0)),
                  pl.BlockSpec((halo, d), lambda i: (jnp.minimum((i + 1) * per, last_halo), 0)),
                  _full(w.shape), _full(b.shape), _full(g.shape), _full(beta.shape)],
        out_specs=pl.BlockSpec((tc, d), lambda i: (i, 0)),
        compiler_params=_cparams("arbitrary"),
        name="dwconv_ln_swish",
    )(glu, glu, glu, w, b, g, beta)


def _post_kernel(kind, has_bias, alpha, *refs):
    refs = list(refs)
    if kind == "lru":
        hf_ref, hb_ref, gate_ref = refs[:3]
        refs = refs[3:]
        core = ((hf_ref[...].astype(F32) + hb_ref[...].astype(F32)) * gate_ref[...].astype(F32)).astype(BF16)
    else:
        core = refs[0][...]
        refs = refs[1:]
    w_ref = refs[0]
    refs = refs[1:]
    y = _dot(core, w_ref[...])
    if has_bias:
        y = y + refs[0][...]
        refs = refs[1:]
    x_ref, mod_ref, lng_ref, lnb_ref, rt_ref, x1_ref, t_ref, aff_ref = refs
    m = mod_ref[0]
    x1 = _layer_norm(alpha * x_ref[...] + m[2:3] * y, lng_ref[...], lnb_ref[...])
    x1_ref[...] = x1
    t = (x1 * (1.0 + m[4:5]) + m[3:4]).astype(BF16)
    t_ref[...] = t
    logits = _dot_nt(rt_ref[...], t)
    e = jnp.exp(logits - jnp.max(logits, axis=0, keepdims=True))
    aff_ref[...] = e / jnp.sum(e, axis=0, keepdims=True)


def _post_call(kind, alpha, cores, w, bias, x_all, mod, ln_g, ln_b, router_t, n_rows, tiles_per_sample,
               n_samples):
    d = x_all.shape[1]
    tm = ROW_TILE
    n_e = router_t.shape[0]
    x_spec, mod_spec = _row_specs(tm, d, tiles_per_sample, n_samples)
    in_specs = [x_spec] * len(cores) + [_full(w.shape)]
    args = list(cores) + [w]
    if bias is not None:
        in_specs.append(_full(bias.shape))
        args.append(bias)
    in_specs += [x_spec, mod_spec, _full(ln_g.shape), _full(ln_b.shape), _full(router_t.shape)]
    args += [x_all, mod, ln_g, ln_b, router_t]
    return pl.pallas_call(
        functools.partial(_post_kernel, kind, bias is not None, alpha),
        out_shape=(jax.ShapeDtypeStruct((n_rows, d), F32),
                   jax.ShapeDtypeStruct((n_rows, d), BF16),
                   jax.ShapeDtypeStruct((n_e, n_rows), F32)),
        grid=(n_rows // tm,),
        in_specs=in_specs,
        out_specs=(x_spec, x_spec, pl.BlockSpec((n_e, tm), lambda i: (0, i))),
        compiler_params=_cparams("arbitrary"),
        name="mixer_out_ln_router",
    )(*args)


def _route_kernel(cap, slot_tile, aff_ref, pos_ref, excl_ref, lo_ref, hi_ref):
    n_e, seq = aff_ref.shape
    tb = min(TOKEN_BLOCK, seq)
    nblk = seq // tb
    bits = pltpu.bitcast(aff_ref[...], I32)

    def search(_, lohi):
        lo, hi = lohi
        mid = lo + jnp.right_shift(hi - lo, 1)
        cnt = jnp.sum((bits >= mid).astype(F32), axis=1, keepdims=True)
        ge = cnt >= cap
        return jnp.where(ge, mid, lo), jnp.where(ge, hi, mid)

    lo0 = jnp.zeros((n_e, 1), I32)
    hi0 = jnp.full((n_e, 1), 0x7F800000, I32)
    thr, _ = lax.fori_loop(0, 31, search, (lo0, hi0))
    n_gt = jnp.sum((bits > thr).astype(F32), axis=1, keepdims=True)
    need = cap - n_gt

    upper = (lax.broadcasted_iota(I32, (tb, tb), 0) < lax.broadcasted_iota(I32, (tb, tb), 1)).astype(BF16)
    lane = lax.broadcasted_iota(I32, (n_e, LANES), 1)
    eq_off = jnp.zeros((n_e, 1), F32)
    pos_off = jnp.zeros((n_e, 1), F32)
    excl = jnp.full((n_e, LANES), float(cap), F32)
    incl = jnp.full((n_e, LANES), 1e9, F32)
    for blk in range(nblk):
        cs = slice(blk * tb, (blk + 1) * tb)
        bb = bits[:, cs]
        gt = bb > thr
        eq = bb == thr
        eqf = jnp.where(eq, 1.0, 0.0)
        rank = _dot(eqf.astype(BF16), upper) + eq_off
        sel = gt | (eq & (rank < need))
        self_f = jnp.where(sel, 1.0, 0.0)
        pin = _dot(self_f.astype(BF16), upper) + pos_off
        pos_ref[:, cs] = jnp.where(sel, pin, -1.0)
        excl = jnp.where(lane == blk, pos_off, excl)
        eq_off = eq_off + jnp.sum(eqf, axis=1, keepdims=True)
        pos_off = pos_off + jnp.sum(self_f, axis=1, keepdims=True)
        incl = jnp.where(lane == blk, pos_off, incl)
    excl_ref[0] = excl.astype(I32)
    lo = jnp.zeros((n_e, LANES), F32)
    hi = jnp.zeros((n_e, LANES), F32)
    for q in range(cap // slot_tile):
        first = float(q * slot_tile)
        last = float(q * slot_tile + slot_tile - 1)
        lo_q = jnp.sum(jnp.where(incl <= first, 1.0, 0.0), axis=1, keepdims=True)
        hi_q = jnp.sum(jnp.where(incl <= last, 1.0, 0.0), axis=1, keepdims=True)
        lo = jnp.where(lane == q, lo_q, lo)
        hi = jnp.where(lane == q, jnp.minimum(hi_q, float(nblk - 1)), hi)
    lo_ref[0] = jnp.broadcast_to(jnp.min(lo, axis=0, keepdims=True), lo.shape).astype(I32)
    hi_ref[0] = jnp.broadcast_to(jnp.max(hi, axis=0, keepdims=True), hi.shape).astype(I32)


def _route_call(aff_t, batch, seq, first_block, cap, slot_tile):
    n_e = aff_t.shape[0]
    assert seq // min(TOKEN_BLOCK, seq) < LANES and cap // slot_tile <= LANES
    meta = jax.ShapeDtypeStruct((batch, n_e, LANES), I32)
    meta_spec = pl.BlockSpec((1, n_e, LANES), lambda b: (b, 0, 0))
    return pl.pallas_call(
        functools.partial(_route_kernel, cap, slot_tile),
        out_shape=(jax.ShapeDtypeStruct((n_e, batch * seq), F32), meta, meta, meta),
        grid=(batch,),
        in_specs=[pl.BlockSpec((n_e, seq), lambda b: (0, first_block + b))],
        out_specs=(pl.BlockSpec((n_e, seq), lambda b: (0, b)), meta_spec, meta_spec, meta_spec),
        compiler_params=_cparams("arbitrary"),
        name="expert_choice_route",
    )(aff_t)


def _dispatch_kernel(lo_ref, hi_ref, t_ref, pos_ref, aff_ref, xg_ref, gate_ref, acc_ref):
    b = pl.program_id(0)
    q = pl.program_id(1)
    n_e, slot_tile, d = xg_ref.shape[1], xg_ref.shape[2], xg_ref.shape[3]
    tb = pos_ref.shape[3]
    step = b * pl.num_programs(1) + q
    slot = (lax.broadcasted_iota(I32, (slot_tile, tb), 0) + q * slot_tile).astype(F32)

    def gather_block(blk):
        hits = [pos_ref[e, blk] == slot for e in range(n_e)]
        onehot = jnp.concatenate([jnp.where(h, 1.0, 0.0).astype(BF16) for h in hits], axis=0)
        x_blk = t_ref[pl.ds(pl.multiple_of(blk * tb, tb), tb), :]
        gates = [jnp.sum(jnp.where(h, aff_ref[e, blk], 0.0), axis=1, keepdims=True)
                 for e, h in enumerate(hits)]
        return _dot(onehot, x_blk), jnp.concatenate(gates, axis=0)

    def body(blk, gacc):
        rows, gates = gather_block(blk)
        acc_ref[...] += rows
        return gacc + gates

    lo = lo_ref[step]
    acc_ref[...], gacc = gather_block(lo)
    gacc = lax.fori_loop(lo + 1, hi_ref[step] + 1, body, gacc)
    for e in range(n_e):
        rows = slice(e * slot_tile, (e + 1) * slot_tile)
        xg_ref[0, e] = acc_ref[rows, :].astype(BF16)
        gate_ref[0, e] = gacc[rows]


def _dispatch_call(t_all, pos, aff_t, lo, hi, batch, seq, first_seq_block, aff_first_block, cap, slot_tile):
    d = t_all.shape[1]
    n_e = pos.shape[0]
    tb = min(TOKEN_BLOCK, seq)
    nblk = seq // tb
    pos4 = pos.reshape(n_e, batch * nblk, 1, tb)
    aff4 = aff_t.reshape(n_e, aff_t.shape[1] // tb, 1, tb)
    grid_spec = pltpu.PrefetchScalarGridSpec(
        num_scalar_prefetch=2,
        grid=(batch, cap // slot_tile),
        in_specs=[
            pl.BlockSpec((seq, d), lambda b, q, *_: (first_seq_block + b, 0), pipeline_mode=pl.Buffered(1)),
            pl.BlockSpec((n_e, nblk, 1, tb), lambda b, q, *_: (0, b, 0, 0)),
            pl.BlockSpec((n_e, nblk, 1, tb), lambda b, q, *_: (0, aff_first_block + b, 0, 0)),
        ],
        out_specs=(pl.BlockSpec((1, n_e, slot_tile, d), lambda b, q, *_: (b, 0, q, 0)),
                   pl.BlockSpec((1, n_e, slot_tile, 1), lambda b, q, *_: (b, 0, q, 0))),
        scratch_shapes=[pltpu.VMEM((n_e * slot_tile, d), F32)],
    )
    return pl.pallas_call(
        _dispatch_kernel,
        out_shape=(jax.ShapeDtypeStruct((batch, n_e, cap, d), BF16),
                   jax.ShapeDtypeStruct((batch, n_e, cap, 1), F32)),
        grid_spec=grid_spec,
        compiler_params=_cparams("arbitrary", "arbitrary"),
        name="moe_dispatch",
    )(lo.reshape(-1), hi.reshape(-1), t_all, pos4, aff4)


def _ffn_kernel(n_streams, w1_ref, w3_ref, w2_ref, *refs):
    ins, outs, accs = refs[:2 * n_streams], refs[2 * n_streams:3 * n_streams], refs[3 * n_streams:]
    f = pl.program_id(2)
    w1 = w1_ref[0, 0].astype(BF16)
    w3 = w3_ref[0, 0].astype(BF16)
    w2 = w2_ref[0, 0].astype(BF16)
    for s in range(n_streams):
        xg_ref, gate_ref, y_ref, acc_ref = ins[2 * s], ins[2 * s + 1], outs[s], accs[s]
        cap = acc_ref.shape[0]
        rt = min(FFN_ROW_TILE, cap)

        def partial_ffn(first, xg_ref=xg_ref, acc_ref=acc_ref, cap=cap, rt=rt):
            for r in range(cap // rt):
                rows = slice(r * rt, (r + 1) * rt)
                xg = xg_ref[0, 0, rows, :]
                h1 = _dot(xg, w1)
                h3 = _dot(xg, w3)
                hid = (h1 * jax.nn.sigmoid(h1) * h3).astype(BF16)
                if first:
                    acc_ref[rows, :] = _dot(hid, w2)
                else:
                    acc_ref[rows, :] += _dot(hid, w2)

        pl.when(f == 0)(functools.partial(partial_ffn, True))
        pl.when(f > 0)(functools.partial(partial_ffn, False))

        @pl.when(f == pl.num_programs(2) - 1)
        def _(acc_ref=acc_ref, gate_ref=gate_ref, y_ref=y_ref):
            y_ref[0, 0] = (acc_ref[...] * gate_ref[0, 0]).astype(BF16)


def _ffn_call(layer, streams, w1, w3, w2):
    n_e, d, ff = w1.shape[1:]
    batch = streams[0][0].shape[0]
    tf = FF_TILE
    tok = lambda e, b, f: (b, e, 0, 0)
    in_specs = [pl.BlockSpec((1, 1, d, tf), lambda e, b, f: (layer, e, 0, f)),
                pl.BlockSpec((1, 1, d, tf), lambda e, b, f: (layer, e, 0, f)),
                pl.BlockSpec((1, 1, tf, d), lambda e, b, f: (layer, e, f, 0))]
    args, out_shape, out_specs, scratch = [w1, w3, w2], [], [], []
    for xg, gate in streams:
        cap = xg.shape[2]
        assert cap % min(FFN_ROW_TILE, cap) == 0
        in_specs += [pl.BlockSpec((1, 1, cap, d), tok), pl.BlockSpec((1, 1, cap, 1), tok)]
        args += [xg, gate]
        out_shape.append(jax.ShapeDtypeStruct(xg.shape, BF16))
        out_specs.append(pl.BlockSpec((1, 1, cap, d), tok))
        scratch.append(pltpu.VMEM((cap, d), F32))
    return pl.pallas_call(
        functools.partial(_ffn_kernel, len(streams)),
        out_shape=tuple(out_shape),
        grid=(n_e, batch, ff // tf),
        in_specs=in_specs,
        out_specs=tuple(out_specs),
        scratch_shapes=scratch,
        compiler_params=_cparams("arbitrary", "arbitrary", "arbitrary"),
        name="expert_ffn",
    )(*args)


def _combine_kernel(n_e, group, nblk, n_tiles, excl_stride, has_alias, alpha, excl_ref, *refs):
    if has_alias:
        refs = refs[1:]
    (x1_ref, pos_ref, mod_ref, lng_ref, lnb_ref, y_hbm, o_ref,
     buf_ref, sem_ref, base_ref, exp_ref, count_ref) = refs
    tile_idx = pl.program_id(0)
    tb, d = x1_ref.shape
    per_tile = COMBINE_TILE_ROWS // group
    max_groups = base_ref.shape[0] // 2

    def group_copy(bb, e, base, par, g):
        return pltpu.make_async_copy(
            y_hbm.at[bb, e, pl.ds(pl.multiple_of(base, group), group)],
            buf_ref.at[par, pl.ds(pl.multiple_of(g * group, group), group)], sem_ref.at[par])

    def issue_tile(tile, par):
        bb, ii = tile // nblk, tile % nblk
        n_groups = jnp.int32(0)
        for e in range(n_e):
            first = excl_ref[(bb * n_e + e) * LANES + ii * excl_stride]
            end = excl_ref[(bb * n_e + e) * LANES + (ii + 1) * excl_stride]
            start = (first // group) * group
            n_run = jnp.where(end > first, (end - start + group - 1) // group, 0)

            def issue(k, g, e=e, start=start):
                base = start + k * group
                base_ref[par * max_groups + g] = base
                exp_ref[par * max_groups + g] = e
                group_copy(bb, e, base, par, g).start()
                return g + 1

            n_groups = lax.fori_loop(0, n_run, issue, n_groups)
        count_ref[par] = n_groups

    @pl.when(tile_idx < n_tiles)
    def _():
        par = tile_idx % 2

        @pl.when(tile_idx == 0)
        def _():
            buf_ref[...] = jnp.zeros(buf_ref.shape, BF16)
            issue_tile(tile_idx, par)

        @pl.when(tile_idx + 1 < n_tiles)
        def _():
            issue_tile(tile_idx + 1, 1 - par)

        n_groups = count_ref[par]

        def wait_one(_, carry):
            group_copy(0, 0, 0, par, 0).wait()
            return carry

        lax.fori_loop(0, n_groups, wait_one, 0)
        row = lax.broadcasted_iota(I32, (group, tb), 0).astype(F32)

        def reduce_groups(first_group, n, f):
            parts = []
            for j in range(n):
                g = first_group + j
                live = g < n_groups
                gc = par * max_groups + jnp.minimum(g, n_groups - 1)
                base = jnp.where(live, base_ref[gc], -2 * group).astype(F32)
                hit = (pos_ref[exp_ref[gc]] - base) == row
                parts.append(jnp.where(hit, 1.0, 0.0).astype(BF16))
            onehot = jnp.concatenate(parts, axis=0)
            rows = buf_ref[par, pl.ds(pl.multiple_of(first_group * group, small * group), n * group), :]
            return f + _dot_tn(onehot, rows)

        small = MXU_DEPTH // group
        n_big = n_groups // per_tile
        f = lax.fori_loop(0, n_big, lambda kt, f: reduce_groups(kt * per_tile, per_tile, f),
                          jnp.zeros((tb, d), F32))
        n_small = (n_groups - n_big * per_tile + small - 1) // small
        f = lax.fori_loop(0, n_small, lambda kt, f: reduce_groups(n_big * per_tile + kt * small, small, f), f)
        m = mod_ref[0]
        o_ref[...] = _layer_norm(alpha * x1_ref[...] + m[5:6] * f, lng_ref[...], lnb_ref[...])

    @pl.when(tile_idx >= n_tiles)
    def _():
        o_ref[...] = x1_ref[...]


def _combine_call(alpha, x1, pos, excl, y, mod, ln_g, ln_b, batch, seq, x_first_block, mod_row, out_rows,
                  alias_buf, carry_first_block=0, carry_blocks=0):
    d = x1.shape[1]
    n_e, cap = y.shape[1], y.shape[2]
    tb = min(COMBINE_TOKENS, seq)
    nblk = seq // tb
    excl_stride = tb // min(TOKEN_BLOCK, seq)
    group = COMBINE_GROUP
    assert cap % group == 0 and COMBINE_TILE_ROWS % group == 0 and group % BF16_ROWS == 0
    assert seq % tb == 0 and tb % min(TOKEN_BLOCK, seq) == 0
    max_groups = n_e * min(tb // group + 1, cap // group)
    buf_rows = -(-max_groups * group // COMBINE_TILE_ROWS) * COMBINE_TILE_ROWS
    has_alias = alias_buf is not None

    n_tiles = batch * nblk

    def row_map(s, *_):
        return (jnp.where(s < n_tiles, x_first_block + s, carry_first_block + (s - n_tiles)), 0)

    def mod_map(s, *_):
        return (jnp.minimum(s // nblk, batch - 1) if mod_row is None else mod_row, 0, 0)

    in_specs = [pl.BlockSpec((tb, d), row_map),
                pl.BlockSpec((n_e, 1, tb), lambda s, *_: (0, 0, jnp.minimum(s, n_tiles - 1))),
                pl.BlockSpec((1, 6, d), mod_map),
                _full(ln_g.shape), _full(ln_b.shape),
                pl.BlockSpec(memory_space=pl.ANY)]
    args = [x1, pos.reshape(n_e, 1, -1), mod, ln_g, ln_b, y]
    aliases = {}
    if has_alias:
        in_specs = [pl.BlockSpec(memory_space=pl.ANY)] + in_specs
        args = [alias_buf] + args
        aliases = {1: 0}
    grid_spec = pltpu.PrefetchScalarGridSpec(
        num_scalar_prefetch=1,
        grid=(n_tiles + carry_blocks,),
        in_specs=in_specs,
        out_specs=pl.BlockSpec((tb, d), row_map),
        scratch_shapes=[pltpu.VMEM((2, buf_rows, d), BF16), pltpu.SemaphoreType.DMA((2,)),
                        pltpu.SMEM((2 * max_groups,), I32), pltpu.SMEM((2 * max_groups,), I32),
                        pltpu.SMEM((2,), I32)],
    )
    return pl.pallas_call(
        functools.partial(_combine_kernel, n_e, group, nblk, n_tiles, excl_stride, has_alias, alpha),
        out_shape=jax.ShapeDtypeStruct((out_rows, d), F32),
        grid_spec=grid_spec,
        input_output_aliases=aliases,
        compiler_params=_cparams("arbitrary"),
        name="moe_combine_ln",
    )(excl.reshape(-1), *args)


def _rope_tables(seq, pad_rows):
    rows = seq // GRID_W
    row = jnp.broadcast_to(jnp.arange(rows, dtype=F32)[:, None], (rows, GRID_W)).reshape(-1)
    col = jnp.broadcast_to(jnp.arange(GRID_W, dtype=F32)[None, :], (rows, GRID_W)).reshape(-1)
    n_freq = HEAD_DIM // 4
    inv = ROPE_BASE ** (-jnp.arange(n_freq, dtype=F32) / n_freq)
    ang = jnp.concatenate([row[:, None] * inv, col[:, None] * inv], axis=-1)
    cos, sin = jnp.cos(ang), jnp.sin(ang)
    reps = LANES // HEAD_DIM
    cos_t = jnp.tile(jnp.concatenate([cos, cos], axis=-1), (1, reps))
    sin_t = jnp.tile(jnp.concatenate([-sin, sin], axis=-1), (1, reps))
    cos_t = jnp.concatenate([cos_t, jnp.ones((pad_rows, LANES), F32)], axis=0)
    sin_t = jnp.concatenate([sin_t, jnp.zeros((pad_rows, LANES), F32)], axis=0)
    return cos_t, sin_t


def _moe(alpha, layer, x1, t, aff_t, mod, ln_g, ln_b, w1, w3, w2, batch, seq, n_ctx, with_ctx, last):
    n_e = aff_t.shape[0]
    n_lat = batch * seq
    cap = max(1, EC_CAPACITY_FACTOR * seq // n_e)
    st = min(SLOT_TILE, cap)
    pos, excl, lo, hi = _route_call(aff_t, batch, seq, 0, cap, st)
    streams = [_dispatch_call(t, pos, aff_t, lo[:, 0, :cap // st], hi[:, 0, :cap // st],
                              batch, seq, 0, 0, cap, st)]
    if with_ctx:
        capc = max(1, EC_CAPACITY_FACTOR * n_ctx // n_e)
        stc = min(SLOT_TILE, capc)
        tbc = min(TOKEN_BLOCK, n_ctx)
        posc, exclc, loc, hic = _route_call(aff_t, batch, n_ctx, n_lat // n_ctx, capc, stc)
        streams.append(_dispatch_call(t, posc, aff_t, loc[:, 0, :capc // stc], hic[:, 0, :capc // stc],
                                      batch, n_ctx, n_lat // n_ctx, n_lat // tbc, capc, stc))
    ys = _ffn_call(layer, streams, w1, w3, w2)
    out_rows = n_lat if last else x1.shape[0]
    tb = min(COMBINE_TOKENS, seq)
    assert last or (batch * n_ctx) % tb == 0
    x2 = _combine_call(alpha, x1, pos, excl, ys[0], mod, ln_g, ln_b, batch, seq, 0, None, out_rows, None,
                       carry_first_block=n_lat // tb, carry_blocks=0 if last else (batch * n_ctx) // tb)
    if with_ctx:
        x2 = _combine_call(alpha, x1, posc, exclc, ys[1], mod, ln_g, ln_b, batch, n_ctx,
                           n_lat // min(COMBINE_TOKENS, n_ctx), batch, x1.shape[0], x2)
    return x2


def kernel(x, c, ctx, c_ctx, ada_w, ada_b, ln1_g, ln1_b, ln2_g, ln2_b, attn_w_qkv, attn_sink, attn_w_o, lru_w_in, lru_conv_w, lru_conv_b, lru_gate_a_w, lru_gate_a_b, lru_gate_x_w, lru_gate_x_b, lru_lambda, lru_w_out, conv_w_in, conv_b_in, conv_dw_w, conv_dw_b, conv_ln_g, conv_ln_b, conv_w_out, conv_b_out, moe_router, moe_w1, moe_w3, moe_w2):
    batch, seq, d = x.shape
    n_ctx = ctx.shape[1]
    depth = ada_w.shape[0]
    n_lat = batch * seq
    rows = n_lat + batch * n_ctx
    tm = ROW_TILE
    assert seq % tm == 0 and (batch * n_ctx) % tm == 0
    tiles_per_sample = seq // tm
    alpha = float((2 * depth) ** 0.25)

    x_all = jnp.concatenate([x.reshape(n_lat, d), ctx.reshape(batch * n_ctx, d)], axis=0)
    cond = jnp.zeros((SUBLANES, d), F32).at[:batch].set(c).at[batch].set(c_ctx)
    mods = _ada_call(cond, ada_w, ada_b)[:, :batch + 1].reshape(depth, batch + 1, 6, d)
    cos_t, sin_t = _rope_tables(seq, tm)
    x_spec, mod_spec = _row_specs(tm, d, tiles_per_sample, batch)
    vec = lambda a: a.reshape(1, -1)

    for i in range(depth):
        last = i == depth - 1
        kind, j = i % 3, i // 3
        mod = mods[i]
        n_rows = rows
        if kind == 0:
            w_qkv = attn_w_qkv[j].astype(BF16)
            kvw = N_KV_HEADS * HEAD_DIM
            rope_spec = pl.BlockSpec(
                (tm, LANES), lambda t: (jnp.where(t < batch * tiles_per_sample, t % tiles_per_sample,
                                                  tiles_per_sample), 0))
            q, k, v = pl.pallas_call(
                _attn_in_kernel,
                out_shape=(jax.ShapeDtypeStruct((rows, d), BF16),
                           jax.ShapeDtypeStruct((rows, kvw), BF16),
                           jax.ShapeDtypeStruct((rows, kvw), BF16)),
                grid=(rows // tm,),
                in_specs=[x_spec, mod_spec, _full(w_qkv.shape), rope_spec, rope_spec],
                out_specs=(x_spec, pl.BlockSpec((tm, kvw), lambda t: (t, 0)),
                           pl.BlockSpec((tm, kvw), lambda t: (t, 0))),
                compiler_params=_cparams("arbitrary"),
                name="attn_in_proj",
            )(x_all, mod, w_qkv, cos_t, sin_t)
            o = _attn_call(q, k, v, vec(attn_sink[j]), batch, seq, n_ctx)
            cores, w_out, b_out, pkind = [o], attn_w_o[j].astype(BF16), None, "direct"
        elif kind == 1:
            w_in = lru_w_in[j].astype(BF16)
            gate, rec = pl.pallas_call(
                _lru_in_kernel,
                out_shape=(jax.ShapeDtypeStruct((rows, d), BF16), jax.ShapeDtypeStruct((rows, d), F32)),
                grid=(rows // tm,),
                in_specs=[x_spec, mod_spec, _full(w_in.shape)],
                out_specs=(x_spec, x_spec),
                compiler_params=_cparams("arbitrary"),
                name="lru_in_proj",
            )(x_all, mod, w_in)
            hf, hb = _lru_call(rec, lru_conv_w[j], vec(lru_conv_b[j]),
                               lru_gate_a_w[j].astype(BF16), lru_gate_a_b[j],
                               lru_gate_x_w[j].astype(BF16), lru_gate_x_b[j], lru_lambda[j],
                               batch, seq, n_ctx)
            cores, w_out, b_out, pkind = [hf, hb, gate], lru_w_out[j].astype(BF16), None, "lru"
        else:
            w_in = conv_w_in[j].astype(BF16)
            glu = pl.pallas_call(
                _conv_in_kernel,
                out_shape=jax.ShapeDtypeStruct((rows, d), F32),
                grid=(rows // tm,),
                in_specs=[x_spec, mod_spec, _full(w_in.shape), _full((1, 2 * d))],
                out_specs=x_spec,
                compiler_params=_cparams("arbitrary"),
                name="conv_in_proj",
            )(x_all, mod, w_in, vec(conv_b_in[j]))
            core = _dwconv_call(glu, conv_dw_w[j], vec(conv_dw_b[j]), vec(conv_ln_g[j]), vec(conv_ln_b[j]),
                                batch, seq, n_ctx)
            cores, w_out, b_out, pkind = [core], conv_w_out[j].astype(BF16), vec(conv_b_out[j]), "direct"
        post_rows = n_lat if last else rows
        x1, t, aff_t = _post_call(pkind, alpha, cores, w_out, b_out, x_all, mod, vec(ln1_g[i]), vec(ln1_b[i]),
                                  moe_router[i].T.astype(BF16), post_rows, tiles_per_sample, batch)
        x_all = _moe(alpha, i, x1, t, aff_t, mod, vec(ln2_g[i]), vec(ln2_b[i]), moe_w1, moe_w3, moe_w2,
                     batch, seq, n_ctx, not last, last)
    return x_all.reshape(batch, seq, d)
```
